```python
import math
import jax, jax.numpy as jnp
from jax import lax
import numpy as np

D_MODEL = 1024
BATCH = 8
SEQ = 2048
DEPTH = 1
DEC_BATCH = 128
DEC_SEQ = 4
PAST_LEN = 2048
PAGE_SIZE = 128

MIX_WIDTH = D_MODEL
ATTN_WIDTH = MIX_WIDTH // 2
CONV_WIDTH = MIX_WIDTH - ATTN_WIDTH
HEAD_DIM = 64
N_DIFF_HEADS = ATTN_WIDTH // (2 * HEAD_DIM)
CONV_KERNEL = 31
D_FF = ((8 * D_MODEL // 3 + 127) // 128) * 128
FFN_KERNEL = 3
Q_BLOCK = 128
LN_EPS = 1e-5
ALPHA = (2 * DEPTH) ** 0.25
BETA = (8 * DEPTH) ** -0.25
IN_COLS = 3 * ATTN_WIDTH + 2 * CONV_WIDTH

kernel_name = 'hymba_diffattn_conformer_convffn_deepnorm_step'


def layer_norm(x, g, b):
    xf = x.astype(jnp.float32)
    mu = jnp.mean(xf, axis=-1, keepdims=True)
    var = jnp.mean(jnp.square(xf - mu), axis=-1, keepdims=True)
    y = (xf - mu) * lax.rsqrt(var + LN_EPS) * g.astype(jnp.float32) + b.astype(jnp.float32)
    return y.astype(x.dtype)


def alibi_slopes(n_heads):
    return 2.0 ** (-8.0 * jnp.arange(1, n_heads + 1, dtype=jnp.float32) / n_heads)


def causal_dwconv(buf, u, w, b):
    full = jnp.concatenate([buf.astype(u.dtype), u], axis=1)
    y = lax.conv_general_dilated(full, w[:, None, :].astype(u.dtype), window_strides=(1,), padding='VALID',
                                 dimension_numbers=('NWC', 'WIO', 'NWC'), feature_group_count=u.shape[-1])
    return y + b.astype(u.dtype), full[:, full.shape[1] - (w.shape[0] - 1):]


def diff_attend(q, k, v, q_pos, k_pos, lam):
    bsz, tq = q.shape[:2]
    tk = k.shape[1]
    s = jnp.einsum('bqnd,bknd->bnqk', q.astype(jnp.float32), k.astype(jnp.float32)) * (HEAD_DIM ** -0.5)
    s = s.reshape(bsz, N_DIFF_HEADS, 2, tq, tk)
    dist = q_pos[:, None] - k_pos[None, :]
    bias = -alibi_slopes(N_DIFF_HEADS)[:, None, None, None] * dist.astype(jnp.float32)
    s = jnp.where(dist >= 0, s + bias, -jnp.inf)
    p = jax.nn.softmax(s, axis=-1)
    a = p[:, :, 0] - lam * p[:, :, 1]
    return jnp.einsum('bhqk,bkhe->bqhe', a, v.astype(jnp.float32))


def prompt_diff_attention(q, k, v, lam):
    bsz, seq = q.shape[:2]
    n_blk = seq // Q_BLOCK
    pos = jnp.arange(seq, dtype=jnp.int32)
    q_blk = q.reshape(bsz, n_blk, Q_BLOCK, 2 * N_DIFF_HEADS, HEAD_DIM).swapaxes(0, 1)
    p_blk = pos.reshape(n_blk, Q_BLOCK)
    o = lax.map(lambda qp: diff_attend(qp[0], k, v, qp[1], pos, lam), (q_blk, p_blk))
    return o.swapaxes(0, 1).reshape(bsz, seq, N_DIFF_HEADS, 2 * HEAD_DIM)


def sample_diff_attention(q, k, v, past_k, past_v, lam):
    past = past_k.shape[1]
    n_new = q.shape[1]
    k_all = jnp.concatenate([past_k.astype(k.dtype), k], axis=1)
    v_all = jnp.concatenate([past_v.astype(v.dtype), v], axis=1)
    q_pos = past + jnp.arange(n_new, dtype=jnp.int32)
    k_pos = jnp.arange(past + n_new, dtype=jnp.int32)
    return diff_attend(q, k_all, v_all, q_pos, k_pos, lam)


def decoder_layer(x, c, depth_idx, past_kv, conv_buf, ffn_buf,
                  w_ada, b_ada, w_in, lambda_q1, lambda_k1, lambda_q2, lambda_k2, subln_g,
                  conv_w, conv_b, conv_ln_g, conv_ln_b, w_out, ln1_g, ln1_b,
                  w_up, ffn_conv_w, ffn_conv_b, w_down, ln2_g, ln2_b):
    bsz, seq = x.shape[:2]
    mod = jax.nn.silu(c) @ w_ada + b_ada
    shift1, scale1, gate1, shift2, scale2, gate2 = jnp.split(mod[:, None, :], 6, axis=-1)

    h = x * (1 + scale1) + shift1
    z = h @ w_in
    q, k, v, glu = jnp.split(z, [ATTN_WIDTH, 2 * ATTN_WIDTH, 3 * ATTN_WIDTH], axis=-1)
    q = q.reshape(bsz, seq, 2 * N_DIFF_HEADS, HEAD_DIM)
    k = k.reshape(bsz, seq, 2 * N_DIFF_HEADS, HEAD_DIM)
    v = v.reshape(bsz, seq, N_DIFF_HEADS, 2 * HEAD_DIM)

    lambda_init = 0.8 - 0.6 * math.exp(-0.3 * depth_idx)
    lam = (jnp.exp(jnp.sum(lambda_q1.astype(jnp.float32) * lambda_k1.astype(jnp.float32)))
           - jnp.exp(jnp.sum(lambda_q2.astype(jnp.float32) * lambda_k2.astype(jnp.float32))) + lambda_init)
    if past_kv is None:
        o = prompt_diff_attention(q, k, v, lam)
    else:
        o = sample_diff_attention(q, k, v, past_kv[0], past_kv[1], lam)
    o = o * lax.rsqrt(jnp.mean(jnp.square(o), axis=-1, keepdims=True) + LN_EPS)
    o = o * subln_g.astype(jnp.float32) * (1.0 - lambda_init)
    attn_out = o.reshape(bsz, seq, ATTN_WIDTH).astype(x.dtype)

    glu_a, glu_g = jnp.split(glu, 2, axis=-1)
    u = glu_a * jax.nn.sigmoid(glu_g)
    cv, new_conv_buf = causal_dwconv(conv_buf, u, conv_w, conv_b)
    conv_out = jax.nn.silu(layer_norm(cv, conv_ln_g, conv_ln_b))

    mix = jnp.concatenate([attn_out, conv_out], axis=-1) @ w_out
    x = layer_norm(ALPHA * x + gate1 * mix, ln1_g, ln1_b)

    h = x * (1 + scale2) + shift2
    up, new_ffn_buf = causal_dwconv(ffn_buf, h @ w_up, ffn_conv_w, ffn_conv_b)
    up_a, up_b = jnp.split(up, 2, axis=-1)
    f = (jax.nn.silu(up_a) * up_b) @ w_down
    x = layer_norm(ALPHA * x + gate2 * f, ln2_g, ln2_b)
    return x, k, v, new_conv_buf, new_ffn_buf


def setup_inputs(seed: int = 0) -> dict:
    key = jax.random.key(seed)
    ks = jax.random.split(key, 32)

    def nrm(k, shape, scale):
        return scale * jax.random.normal(k, shape, jnp.float32)

    d = D_MODEL
    n_pages = PAST_LEN // PAGE_SIZE
    n_used = DEC_BATCH * n_pages
    n_phys = n_used + max(1, n_used // 4)
    page_table = jax.random.permutation(ks[6], n_phys)[:n_used].reshape(DEC_BATCH, n_pages).astype(jnp.int32)
    col_scale = jnp.concatenate([jnp.ones((2 * ATTN_WIDTH,), jnp.float32),
                                 jnp.full((ATTN_WIDTH + CONV_WIDTH,), BETA, jnp.float32),
                                 jnp.ones((CONV_WIDTH,), jnp.float32)])
    return {
        'x_prompt': nrm(ks[0], (BATCH, SEQ, d), 1.0),
        'x_sample': nrm(ks[1], (DEC_BATCH, DEC_SEQ, d), 1.0),
        'c_prompt': nrm(ks[2], (BATCH, d), 1.0),
        'c_sample': nrm(ks[3], (DEC_BATCH, d), 1.0),
        'cache_k': nrm(ks[4], (DEPTH, n_phys, PAGE_SIZE, 2 * N_DIFF_HEADS, HEAD_DIM), 1.0),
        'cache_v': nrm(ks[5], (DEPTH, n_phys, PAGE_SIZE, N_DIFF_HEADS, 2 * HEAD_DIM), BETA),
        'page_table': page_table,
        'state_conv': nrm(ks[7], (DEPTH, DEC_BATCH, CONV_KERNEL - 1, CONV_WIDTH), 0.5 * BETA),
        'state_ffn': nrm(ks[8], (DEPTH, DEC_BATCH, FFN_KERNEL - 1, 2 * D_FF), BETA),
        'ln_emb_g': 1.0 + nrm(ks[9], (d,), 0.02),
        'ln_emb_b': nrm(ks[10], (d,), 0.02),
        'w_ada': nrm(ks[11], (DEPTH, d, 6 * d), 0.5 * d ** -0.5),
        'b_ada': nrm(ks[12], (DEPTH, 6 * d), 0.01),
        'w_in': nrm(ks[13], (DEPTH, d, IN_COLS), d ** -0.5) * col_scale,
        'lambda_q1': nrm(ks[14], (DEPTH, HEAD_DIM), 0.1),
        'lambda_k1': nrm(ks[15], (DEPTH, HEAD_DIM), 0.1),
        'lambda_q2': nrm(ks[16], (DEPTH, HEAD_DIM), 0.1),
        'lambda_k2': nrm(ks[17], (DEPTH, HEAD_DIM), 0.1),
        'subln_g': 1.0 + nrm(ks[18], (DEPTH, 2 * HEAD_DIM), 0.02),
        'conv_w': nrm(ks[19], (DEPTH, CONV_KERNEL, CONV_WIDTH), CONV_KERNEL ** -0.5),
        'conv_b': nrm(ks[20], (DEPTH, CONV_WIDTH), 0.02),
        'conv_ln_g': 1.0 + nrm(ks[21], (DEPTH, CONV_WIDTH), 0.02),
        'conv_ln_b': nrm(ks[22], (DEPTH, CONV_WIDTH), 0.02),
        'w_out': nrm(ks[23], (DEPTH, MIX_WIDTH, d), BETA * MIX_WIDTH ** -0.5),
        'ln1_g': 1.0 + nrm(ks[24], (DEPTH, d), 0.02),
        'ln1_b': nrm(ks[25], (DEPTH, d), 0.02),
        'w_up': nrm(ks[26], (DEPTH, d, 2 * D_FF), BETA * d ** -0.5),
        'ffn_conv_w': nrm(ks[27], (DEPTH, FFN_KERNEL, 2 * D_FF), FFN_KERNEL ** -0.5),
        'ffn_conv_b': nrm(ks[28], (DEPTH, 2 * D_FF), 0.02),
        'w_down': nrm(ks[29], (DEPTH, D_FF, d), BETA * D_FF ** -0.5),
        'ln2_g': 1.0 + nrm(ks[30], (DEPTH, d), 0.02),
        'ln2_b': nrm(ks[31], (DEPTH, d), 0.02),
    }


def reference(x_prompt, x_sample, c_prompt, c_sample, cache_k, cache_v, page_table, state_conv, state_ffn,
              ln_emb_g, ln_emb_b, w_ada, b_ada, w_in, lambda_q1, lambda_k1, lambda_q2, lambda_k2, subln_g,
              conv_w, conv_b, conv_ln_g, conv_ln_b, w_out, ln1_g, ln1_b,
              w_up, ffn_conv_w, ffn_conv_b, w_down, ln2_g, ln2_b):
    xp = layer_norm(x_prompt, ln_emb_g, ln_emb_b)
    xs = layer_norm(x_sample, ln_emb_g, ln_emb_b)
    bsz = x_prompt.shape[0]
    n_seq = x_sample.shape[0]
    kp, vp, cp, fp = [], [], [], []
    ks_, vs_, cs_, fs_ = [], [], [], []
    for l in range(DEPTH):
        wl = (w_ada[l], b_ada[l], w_in[l], lambda_q1[l], lambda_k1[l], lambda_q2[l], lambda_k2[l], subln_g[l],
              conv_w[l], conv_b[l], conv_ln_g[l], conv_ln_b[l], w_out[l], ln1_g[l], ln1_b[l],
              w_up[l], ffn_conv_w[l], ffn_conv_b[l], w_down[l], ln2_g[l], ln2_b[l])
        zero_conv = jnp.zeros((bsz, CONV_KERNEL - 1, CONV_WIDTH), xp.dtype)
        zero_ffn = jnp.zeros((bsz, FFN_KERNEL - 1, 2 * D_FF), xp.dtype)
        xp, k_new, v_new, c_new, f_new = decoder_layer(xp, c_prompt, l, None, zero_conv, zero_ffn, *wl)
        kp.append(k_new)
        vp.append(v_new)
        cp.append(c_new)
        fp.append(f_new)
        past_k = cache_k[l][page_table].reshape(n_seq, -1, 2 * N_DIFF_HEADS, HEAD_DIM)
        past_v = cache_v[l][page_table].reshape(n_seq, -1, N_DIFF_HEADS, 2 * HEAD_DIM)
        xs, k_new, v_new, c_new, f_new = decoder_layer(xs, c_sample, l, (past_k, past_v),
                                                       state_conv[l], state_ffn[l], *wl)
        ks_.append(k_new)
        vs_.append(v_new)
        cs_.append(c_new)
        fs_.append(f_new)
    return (xp, xs, jnp.stack(kp), jnp.stack(vp), jnp.stack(cp), jnp.stack(fp),
            jnp.stack(ks_), jnp.stack(vs_), jnp.stack(cs_), jnp.stack(fs_))
```

```python
import functools
import math

import jax
import jax.numpy as jnp
from jax import lax
from jax.experimental import pallas as pl
from jax.experimental.pallas import tpu as pltpu

F32 = jnp.float32
BF16 = jnp.bfloat16

LN_EPS = 1e-5
HEAD_DIM = 64
HEAD_LANES = 2 * HEAD_DIM
LANE = 128
SUBLANE = 8
VMEM_LIMIT = 56 * 1024 * 1024
NEG_BIG = -1e30


def _ln(x, g, b):
    mu = jnp.mean(x, axis=-1, keepdims=True)
    xc = x - mu
    var = jnp.mean(xc * xc, axis=-1, keepdims=True)
    return xc * lax.rsqrt(var + LN_EPS) * g + b


def _silu(x):
    return x * jax.nn.sigmoid(x)


def _dot(a, b):
    return jnp.dot(a, b, preferred_element_type=F32)


def _dot_nt(a, b):
    return lax.dot_general(a, b, (((1,), (1,)), ((), ())), preferred_element_type=F32)


def _const_spec(shape):
    nd = len(shape)
    return pl.BlockSpec(shape, lambda *_: (0,) * nd, pipeline_mode=pl.Buffered(1))


def _alibi_slope(h, n_heads):
    return 2.0 ** (-8.0 * (h + 1) / n_heads)


def _mod_kernel(c_ref, w_ref, b_ref, o_ref):
    a = _silu(c_ref[...])
    o_ref[...] = jnp.dot(a, w_ref[...], precision=lax.Precision.HIGHEST,
                         preferred_element_type=F32) + b_ref[...]


def _modulation(c_all, w_ada, b_ada):
    rows, d = c_all.shape
    n_out = w_ada.shape[1]
    return pl.pallas_call(
        _mod_kernel,
        grid=(n_out // d,),
        in_specs=[pl.BlockSpec((rows, d), lambda j: (0, 0)),
                  pl.BlockSpec((d, d), lambda j: (0, j)),
                  pl.BlockSpec((1, d), lambda j: (0, j))],
        out_specs=pl.BlockSpec((rows, d), lambda j: (0, j)),
        out_shape=jax.ShapeDtypeStruct((rows, n_out), F32),
        compiler_params=pltpu.CompilerParams(vmem_limit_bytes=VMEM_LIMIT),
        name="modulation",
    )(c_all, w_ada, b_ada.reshape(1, n_out))


def _inproj_kernel(x_ref, sh_ref, sc_ref, lg_ref, lb_ref, w_ref,
                   q_ref, kt_ref, ktb_ref, v_ref, vb_ref, u_ref, *maybe_k_ref, width):
    xp = _ln(x_ref[...], lg_ref[...], lb_ref[...])
    h = (xp * (1.0 + sc_ref[...]) + sh_ref[...]).astype(BF16)

    def proj(i):
        return _dot(h, w_ref[:, i * width:(i + 1) * width])

    q = proj(0)
    q_ref[...] = (q * (HEAD_DIM ** -0.5)).astype(BF16)
    k = proj(1)
    kt = k.T
    kt_ref[...] = kt
    ktb_ref[...] = kt.astype(BF16)
    if maybe_k_ref:
        maybe_k_ref[0][...] = k
    v = proj(2)
    v_ref[...] = v
    vb_ref[...] = v.astype(BF16)
    u_ref[...] = proj(3) * jax.nn.sigmoid(proj(4))


def _in_projection(x2d, mod, ln_g, ln_b, w_in_bf, *, groups, tm, emit_k_rows):
    n, d = x2d.shape
    width = w_in_bf.shape[1] // 5
    per_group = n // groups // tm
    mod_groups, r = mod.shape[0], mod.shape[1]

    def mod_spec(comp):
        return pl.BlockSpec((None, r, d),
                            lambda i: ((i // per_group) % mod_groups, 0, comp))

    row_spec = lambda w: pl.BlockSpec((tm, w), lambda i: (i, 0))
    kt_spec = pl.BlockSpec((None, width, tm), lambda i: (i // per_group, 0, i % per_group))
    kt_shape = (groups, width, n // groups)
    out_specs = [row_spec(width), kt_spec, kt_spec, row_spec(width), row_spec(width),
                 row_spec(width)]
    out_shape = [jax.ShapeDtypeStruct((n, width), BF16),
                 jax.ShapeDtypeStruct(kt_shape, F32),
                 jax.ShapeDtypeStruct(kt_shape, BF16),
                 jax.ShapeDtypeStruct((n, width), F32),
                 jax.ShapeDtypeStruct((n, width), BF16),
                 jax.ShapeDtypeStruct((n, width), F32)]
    if emit_k_rows:
        out_specs.append(row_spec(width))
        out_shape.append(jax.ShapeDtypeStruct((n, width), F32))
    return pl.pallas_call(
        functools.partial(_inproj_kernel, width=width),
        grid=(n // tm,),
        in_specs=[row_spec(d), mod_spec(0), mod_spec(1),
                  _const_spec((1, d)), _const_spec((1, d)),
                  _const_spec(w_in_bf.shape)],
        out_specs=out_specs,
        out_shape=out_shape,
        compiler_params=pltpu.CompilerParams(
            dimension_semantics=("arbitrary",), vmem_limit_bytes=VMEM_LIMIT),
        name="in_projection",
    )(x2d, mod, mod, ln_g.reshape(1, d), ln_b.reshape(1, d), w_in_bf)


def _lambda_value(lq1_ref, lk1_ref, lq2_ref, lk2_ref, lambda_init):
    s1 = jnp.sum(lq1_ref[...] * lk1_ref[...], axis=-1, keepdims=True)
    s2 = jnp.sum(lq2_ref[...] * lk2_ref[...], axis=-1, keepdims=True)
    return jnp.exp(s1) - jnp.exp(s2) + lambda_init


def _sub_ln(o, g, lambda_init):
    ms = jnp.mean(o * o, axis=-1, keepdims=True)
    return o * lax.rsqrt(ms + LN_EPS) * g * (1.0 - lambda_init)


def _prompt_attn_kernel(q_ref, kt_ref, v_ref, lq1_ref, lk1_ref, lq2_ref, lk2_ref, g_ref,
                        o_ref, m_scr, l_scr, acc_scr, *, n_heads, tq, tk, lambda_init):
    qi = pl.program_id(1)
    ki = pl.program_id(2)

    @pl.when(ki == 0)
    def _init():
        m_scr[...] = jnp.full(m_scr.shape, NEG_BIG, F32)
        l_scr[...] = jnp.zeros(l_scr.shape, F32)
        acc_scr[...] = jnp.zeros(acc_scr.shape, F32)

    @pl.when(ki <= qi)
    def _compute():
        lane = lax.broadcasted_iota(jnp.int32, (1, HEAD_LANES), 1)
        first_map = lane < HEAD_DIM
        kpos = (ki * tk + lax.broadcasted_iota(jnp.int32, (1, tk), 1)).astype(F32)
        row = lax.broadcasted_iota(jnp.int32, (2 * tq, tk), 0)
        col = lax.broadcasted_iota(jnp.int32, (2 * tq, tk), 1)
        qrow = jnp.where(row >= tq, row - tq, row)
        visible = col + (ki - qi) * tk <= qrow
        for h in range(n_heads):
            cols = slice(h * HEAD_LANES, (h + 1) * HEAD_LANES)
            qh = q_ref[:, cols]
            zero = jnp.zeros_like(qh)
            qbd = jnp.concatenate([jnp.where(first_map, qh, zero),
                                   jnp.where(first_map, zero, qh)], axis=0)
            s = _dot(qbd, kt_ref[cols, :])
            s = s + _alibi_slope(h, n_heads) * kpos
            s = jnp.where(visible, s, NEG_BIG)
            m_prev = m_scr[h]
            m_new = jnp.maximum(m_prev, jnp.max(s, axis=-1, keepdims=True))
            alpha = jnp.exp(m_prev - m_new)
            p = jnp.exp(s - m_new[:, :1])
            l_scr[h] = alpha * l_scr[h] + jnp.sum(p, axis=-1, keepdims=True)
            acc_scr[h] = alpha * acc_scr[h] + _dot(p.astype(BF16), v_ref[:, cols])
            m_scr[h] = m_new

    @pl.when(ki == qi)
    def _finalize():
        lam = _lambda_value(lq1_ref, lk1_ref, lq2_ref, lk2_ref, lambda_init)
        for h in range(n_heads):
            o = acc_scr[h] / l_scr[h]
            o = o[:tq] - lam * o[tq:]
            o_ref[:, h * HEAD_LANES:(h + 1) * HEAD_LANES] = _sub_ln(
                o, g_ref[...], lambda_init).astype(o_ref.dtype)


def _prompt_attention(q, ktb, vb, lam_params, subln_g, bsz, seq, lambda_init, tq=512):
    n, width = q.shape
    n_heads = width // HEAD_LANES
    tk = tq
    nq = seq // tq
    lq1, lk1, lq2, lk2 = [p.reshape(1, HEAD_DIM) for p in lam_params]
    kt_spec = pl.BlockSpec((None, width, tk), lambda b, i, j: (b, 0, jnp.minimum(i, j)))
    v_spec = pl.BlockSpec((tk, width), lambda b, i, j: (b * nq + jnp.minimum(i, j), 0))
    q_spec = pl.BlockSpec((tq, width), lambda b, i, j: (b * nq + i, 0))
    small = lambda w: pl.BlockSpec((1, w), lambda b, i, j: (0, 0))
    return pl.pallas_call(
        functools.partial(_prompt_attn_kernel, n_heads=n_heads, tq=tq, tk=tk,
                          lambda_init=lambda_init),
        grid=(bsz, nq, nq),
        in_specs=[q_spec, kt_spec, v_spec,
                  small(HEAD_DIM), small(HEAD_DIM), small(HEAD_DIM), small(HEAD_DIM),
                  small(HEAD_LANES)],
        out_specs=q_spec,
        out_shape=jax.ShapeDtypeStruct((n, width), BF16),
        scratch_shapes=[pltpu.VMEM((n_heads, 2 * tq, LANE), F32),
                        pltpu.VMEM((n_heads, 2 * tq, LANE), F32),
                        pltpu.VMEM((n_heads, 2 * tq, HEAD_LANES), F32)],
        compiler_params=pltpu.CompilerParams(
            dimension_semantics=("arbitrary", "arbitrary", "arbitrary"),
            vmem_limit_bytes=VMEM_LIMIT),
        name="prompt_attention",
    )(q, ktb, vb, lq1, lk1, lq2, lk2, subln_g.reshape(1, HEAD_LANES))


def _sample_attn_kernel(pt_ref, q_ref, kn_ref, vn_ref, lq1_ref, lk1_ref, lq2_ref, lk2_ref,
                        g_ref, *rest, n_pages, page, n_new, n_heads, lambda_init):
    kt_refs = rest[:n_pages]
    v_refs = rest[n_pages:2 * n_pages]
    o_ref = rest[2 * n_pages]
    s_scr = rest[2 * n_pages + 1]
    del pt_ref
    width = n_heads * HEAD_LANES
    n_maps = 2 * n_heads
    rows = n_new * n_maps
    past = n_pages * page

    q = q_ref[0].astype(F32)
    mp = lax.broadcasted_iota(jnp.int32, (n_maps, width), 0)
    cl = lax.broadcasted_iota(jnp.int32, (n_maps, width), 1)
    map_lanes = (cl // HEAD_DIM) == mp
    qbd32 = jnp.concatenate(
        [jnp.where(map_lanes, jnp.broadcast_to(q[i:i + 1, :], (n_maps, width)), 0.0)
         for i in range(n_new)], axis=0)
    qbd = qbd32.astype(BF16)

    for p in range(n_pages):
        s_scr[:, p * page:(p + 1) * page] = _dot(qbd, kt_refs[p][...].astype(BF16))

    rmap = lax.broadcasted_iota(jnp.int32, (rows, 1), 0) % n_maps
    rqry = lax.broadcasted_iota(jnp.int32, (rows, 1), 0) // n_maps
    slope = jnp.zeros((rows, 1), F32)
    for h in range(n_heads):
        slope = jnp.where(rmap // 2 == h, _alibi_slope(h, n_heads), slope)
    tpos = lax.broadcasted_iota(jnp.int32, (1, past), 1).astype(F32)
    s = s_scr[...] + slope * tpos

    kn = kn_ref[0]
    vn = vn_ref[0]
    s_new = []
    for j in range(n_new):
        sj = jnp.sum(qbd32 * kn[j:j + 1, :], axis=-1, keepdims=True) + slope * float(past + j)
        s_new.append(jnp.where(rqry >= j, sj, NEG_BIG))
    m = jnp.max(s, axis=-1, keepdims=True)
    for sj in s_new:
        m = jnp.maximum(m, sj)
    pr = jnp.exp(s - m)
    l = jnp.sum(pr, axis=-1, keepdims=True)
    prb = pr.astype(BF16)
    p_new = []
    for sj in s_new:
        pj = jnp.exp(sj - m)
        l = l + pj
        p_new.append(pj)
    inv_l = 1.0 / l

    lam = _lambda_value(lq1_ref, lk1_ref, lq2_ref, lk2_ref, lambda_init)
    mrow = lax.broadcasted_iota(jnp.int32, (n_maps, 1), 0)
    for h in range(n_heads):
        cols = slice(h * HEAD_LANES, (h + 1) * HEAD_LANES)
        out = jnp.zeros((rows, HEAD_LANES), F32)
        for j, pj in enumerate(p_new):
            out = out + pj * vn[j:j + 1, cols]
        for p in range(n_pages):
            vh = v_refs[p][pl.ds(h, page, stride=n_heads), :].astype(BF16)
            out = out + _dot(prb[:, p * page:(p + 1) * page], vh)
        out = out * inv_l
        weight = jnp.where(mrow == 2 * h, 1.0, jnp.where(mrow == 2 * h + 1, -lam, 0.0))
        for i in range(n_new):
            o = jnp.sum(out[i * n_maps:(i + 1) * n_maps, :] * weight, axis=0, keepdims=True)
            o_ref[0, i:i + 1, cols] = _sub_ln(o, g_ref[...], lambda_init)
    o_ref[0, n_new:, :] = jnp.zeros((SUBLANE - n_new, width), F32)


def _sample_attention(q8, kn8, vn8, cache_kt, cache_v, page_table, lam_params, subln_g,
                      n_new, lambda_init):
    n_seq, _, width = q8.shape
    n_pages = page_table.shape[1]
    page = cache_kt.shape[2]
    n_heads = width // HEAD_LANES
    assert 2 * n_heads == SUBLANE and n_new <= SUBLANE
    lq1, lk1, lq2, lk2 = [p.reshape(1, HEAD_DIM) for p in lam_params]
    seq_spec = pl.BlockSpec((1, SUBLANE, width), lambda s, pt: (s, 0, 0))
    small = lambda w: pl.BlockSpec((1, w), lambda s, pt: (0, 0))

    def page_spec(p, shape):
        return pl.BlockSpec((None,) + shape, lambda s, pt: (pt[s * n_pages + p], 0, 0))

    grid_spec = pltpu.PrefetchScalarGridSpec(
        num_scalar_prefetch=1,
        grid=(n_seq,),
        in_specs=[seq_spec, seq_spec, seq_spec,
                  small(HEAD_DIM), small(HEAD_DIM), small(HEAD_DIM), small(HEAD_DIM),
                  small(HEAD_LANES)]
                 + [page_spec(p, cache_kt.shape[1:]) for p in range(n_pages)]
                 + [page_spec(p, cache_v.shape[1:]) for p in range(n_pages)],
        out_specs=seq_spec,
        scratch_shapes=[pltpu.VMEM((n_new * 2 * n_heads, n_pages * page), F32)],
    )
    return pl.pallas_call(
        functools.partial(_sample_attn_kernel, n_pages=n_pages, page=page, n_new=n_new,
                          n_heads=n_heads, lambda_init=lambda_init),
        grid_spec=grid_spec,
        out_shape=jax.ShapeDtypeStruct((n_seq, SUBLANE, width), F32),
        compiler_params=pltpu.CompilerParams(
            dimension_semantics=("arbitrary",), vmem_limit_bytes=VMEM_LIMIT),
        name="sample_attention",
    )(page_table.reshape(-1), q8, kn8, vn8, lq1, lk1, lq2, lk2,
      subln_g.reshape(1, HEAD_LANES), *([cache_kt] * n_pages), *([cache_v] * n_pages))


def _conv_epilogue(cv, lg_ref, lb_ref, o_ref):
    o_ref[...] = _silu(_ln(cv, lg_ref[...], lb_ref[...])).astype(o_ref.dtype)


def _prompt_conv_kernel(u_ref, w_ref, b_ref, lg_ref, lb_ref, o_ref, buf, *, tt, taps, halo):
    t = pl.program_id(1)

    @pl.when(t == 0)
    def _zero_history():
        buf[0:halo, :] = jnp.zeros((halo, buf.shape[1]), F32)

    buf[halo:halo + tt, :] = u_ref[...]
    base = halo - (taps - 1)
    cv = jnp.broadcast_to(b_ref[...], (tt, buf.shape[1]))
    for j in range(taps):
        cv = cv + w_ref[j:j + 1, :] * buf[base + j:base + j + tt, :]
    _conv_epilogue(cv, lg_ref, lb_ref, o_ref)
    buf[0:halo, :] = buf[tt:tt + halo, :]


def _prompt_conv(u, conv_w, conv_b, ln_g, ln_b, bsz, seq, tt=256):
    n, width = u.shape
    taps = conv_w.shape[0]
    halo = -(-(taps - 1) // SUBLANE) * SUBLANE
    nt = seq // tt
    row_spec = pl.BlockSpec((tt, width), lambda b, t: (b * nt + t, 0))
    const = lambda r: pl.BlockSpec((r, width), lambda b, t: (0, 0))
    return pl.pallas_call(
        functools.partial(_prompt_conv_kernel, tt=tt, taps=taps, halo=halo),
        grid=(bsz, nt),
        in_specs=[row_spec, const(taps), const(1), const(1), const(1)],
        out_specs=row_spec,
        out_shape=jax.ShapeDtypeStruct((n, width), BF16),
        scratch_shapes=[pltpu.VMEM((halo + tt, width), F32)],
        compiler_params=pltpu.CompilerParams(
            dimension_semantics=("arbitrary", "arbitrary"), vmem_limit_bytes=VMEM_LIMIT),
        name="prompt_conv",
    )(u, conv_w, conv_b.reshape(1, width), ln_g.reshape(1, width), ln_b.reshape(1, width))


def _sample_conv_kernel(full_ref, w_ref, b_ref, lg_ref, lb_ref, o_ref, *, n_new, n_seq, taps):
    for t in range(n_new):
        cv = jnp.broadcast_to(b_ref[...], (n_seq, full_ref.shape[2]))
        for j in range(taps):
            cv = cv + w_ref[j:j + 1, :] * full_ref[t + j]
        o_ref[t] = _silu(_ln(cv, lg_ref[...], lb_ref[...])).astype(o_ref.dtype)


def _sample_conv(full_tm, conv_w, conv_b, ln_g, ln_b, n_new):
    n_time, n_seq, width = full_tm.shape
    taps = conv_w.shape[0]
    return pl.pallas_call(
        functools.partial(_sample_conv_kernel, n_new=n_new, n_seq=n_seq, taps=taps),
        out_shape=jax.ShapeDtypeStruct((n_new, n_seq, width), BF16),
        compiler_params=pltpu.CompilerParams(vmem_limit_bytes=VMEM_LIMIT),
        name="sample_conv",
    )(full_tm, conv_w, conv_b.reshape(1, width), ln_g.reshape(1, width), ln_b.reshape(1, width))


def _ffn_kernel(*refs, has_state, tm, pad, tshift, fc, d_ff, alpha):
    (x_ref, attn_ref, conv_ref, g1_ref, sh2_ref, sc2_ref, g2_ref, leg_ref, leb_ref,
     wout_ref, l1g_ref, l1b_ref, wup_ref, cw_ref, cb_ref, wdn_ref, l2g_ref, l2b_ref) = refs[:18]
    refs = refs[18:]
    if has_state:
        state_ref, refs = refs[0], refs[1:]
    y_ref, tail_ref, ubuf_a, ubuf_b, acc_ref = refs
    half = attn_ref.shape[1]

    xp = _ln(x_ref[...], leg_ref[...], leb_ref[...])
    mix = _dot(attn_ref[...], wout_ref[0:half, :]) + _dot(conv_ref[...], wout_ref[half:, :])
    x1 = _ln(alpha * xp + g1_ref[...] * mix, l1g_ref[...], l1b_ref[...])
    h2 = (x1 * (1.0 + sc2_ref[...]) + sh2_ref[...]).astype(BF16)

    if has_state:
        hist_ref = state_ref
    else:
        hist_ref = tail_ref

        @pl.when(pl.program_id(1) == 0)
        def _zero_history():
            tail_ref[...] = jnp.zeros(tail_ref.shape, F32)

    def conv_part(col, ubuf):
        cols = slice(col, col + fc)
        up = _dot(h2, wup_ref[:, cols])
        ubuf[0:pad, :] = hist_ref[:, cols]
        ubuf[pad:pad + tm, :] = up
        tail_ref[:, cols] = up[tm - pad:, :]
        return (cw_ref[0:1, cols] * ubuf[pad - 2 * tshift:pad - 2 * tshift + tm, :]
                + cw_ref[1:2, cols] * ubuf[pad - tshift:pad - tshift + tm, :]
                + cw_ref[2:3, cols] * up + cb_ref[:, cols])

    for c in range(d_ff // fc):
        ca = conv_part(c * fc, ubuf_a)
        cb = conv_part(d_ff + c * fc, ubuf_b)
        g = (_silu(ca) * cb).astype(BF16)
        contrib = _dot(g, wdn_ref[c * fc:(c + 1) * fc, :])
        if c == 0:
            acc_ref[...] = contrib
        else:
            acc_ref[...] += contrib

    y_ref[...] = _ln(alpha * x1 + g2_ref[...] * acc_ref[...], l2g_ref[...], l2b_ref[...])


def _ffn(x2d, attn, conv, mod, ln_emb, w_out_bf, ln1, w_up_bf, ffn_conv_w, ffn_conv_b,
         w_down_bf, ln2, *, groups, tiles, tm, pad, tshift, state, alpha, fc=256):
    n, d = x2d.shape
    half = attn.shape[1]
    d_ff = w_down_bf.shape[0]
    r = mod.shape[1]
    has_state = state is not None
    assert pad == 2 * tshift or not has_state
    row = lambda w: pl.BlockSpec((tm, w), lambda b, t: (b * tiles + t, 0))
    mod_spec = lambda comp: pl.BlockSpec((None, r, d), lambda b, t: (b, 0, comp))
    vec = lambda a: a.reshape(1, -1)
    args = [x2d, attn, conv, mod, mod, mod, mod, vec(ln_emb[0]), vec(ln_emb[1]),
            w_out_bf, vec(ln1[0]), vec(ln1[1]), w_up_bf, ffn_conv_w, vec(ffn_conv_b),
            w_down_bf, vec(ln2[0]), vec(ln2[1])]
    in_specs = [row(d), row(half), row(half), mod_spec(2), mod_spec(3), mod_spec(4), mod_spec(5),
                _const_spec((1, d)), _const_spec((1, d)),
                _const_spec(w_out_bf.shape), _const_spec((1, d)), _const_spec((1, d)),
                _const_spec(w_up_bf.shape), _const_spec(ffn_conv_w.shape),
                _const_spec((1, 2 * d_ff)), _const_spec(w_down_bf.shape),
                _const_spec((1, d)), _const_spec((1, d))]
    if has_state:
        args.append(state)
        in_specs.append(_const_spec(state.shape))
    y, tail = pl.pallas_call(
        functools.partial(_ffn_kernel, has_state=has_state, tm=tm, pad=pad, tshift=tshift,
                          fc=fc, d_ff=d_ff, alpha=alpha),
        grid=(groups, tiles),
        in_specs=in_specs,
        out_specs=[row(d), pl.BlockSpec((None, pad, 2 * d_ff), lambda b, t: (b, 0, 0))],
        out_shape=[jax.ShapeDtypeStruct((n, d), F32),
                   jax.ShapeDtypeStruct((groups, pad, 2 * d_ff), F32)],
        scratch_shapes=[pltpu.VMEM((pad + tm, fc), F32), pltpu.VMEM((pad + tm, fc), F32),
                        pltpu.VMEM((tm, d), F32)],
        compiler_params=pltpu.CompilerParams(
            dimension_semantics=("arbitrary", "arbitrary"), vmem_limit_bytes=VMEM_LIMIT),
        name="ffn",
    )(*args)
    return y, tail


def kernel(x_prompt, x_sample, c_prompt, c_sample, cache_k, cache_v, page_table, state_conv, state_ffn, ln_emb_g, ln_emb_b, w_ada, b_ada, w_in, lambda_q1, lambda_k1, lambda_q2, lambda_k2, subln_g, conv_w, conv_b, conv_ln_g, conv_ln_b, w_out, ln1_g, ln1_b, w_up, ffn_conv_w, ffn_conv_b, w_down, ln2_g, ln2_b):
    bsz, seq, d = x_prompt.shape
    n_seq, n_new, _ = x_sample.shape
    depth = w_ada.shape[0]
    assert depth == 1, "the prompt/sample activations are threaded for a single layer"
    n_phys, page = cache_k.shape[1], cache_k.shape[2]
    width = cache_k.shape[3] * cache_k.shape[4]
    d_ff = w_down.shape[1]
    conv_taps = conv_w.shape[1]
    ffn_taps = ffn_conv_w.shape[1]
    assert ffn_taps == 3
    alpha = (2 * depth) ** 0.25
    tm = 512
    n_rows_s = n_seq * n_new

    l = 0
    lambda_init = 0.8 - 0.6 * math.exp(-0.3 * l)
    lam_params = (lambda_q1[l], lambda_k1[l], lambda_q2[l], lambda_k2[l])
    w_in_bf = w_in[l].astype(BF16)
    w_out_bf = w_out[l].astype(BF16)
    w_up_bf = w_up[l].astype(BF16)
    w_down_bf = w_down[l].astype(BF16)

    mod = _modulation(jnp.concatenate([c_prompt, c_sample], axis=0), w_ada[l], b_ada[l])
    mod_p = mod[:bsz].reshape(bsz, 1, 6 * d)
    mod_s = mod[bsz:]

    n_kh = width // HEAD_DIM
    xp2d = x_prompt.reshape(bsz * seq, d)
    q, kt, ktb, v, vb, u = _in_projection(xp2d, mod_p, ln_emb_g, ln_emb_b, w_in_bf,
                                          groups=bsz, tm=tm, emit_k_rows=False)
    attn = _prompt_attention(q, ktb, vb, lam_params, subln_g[l], bsz, seq, lambda_init)
    conv = _prompt_conv(u, conv_w[l], conv_b[l], conv_ln_g[l], conv_ln_b[l], bsz, seq)
    y_p, tail_p = _ffn(xp2d, attn, conv, mod_p, (ln_emb_g, ln_emb_b), w_out_bf,
                       (ln1_g[l], ln1_b[l]), w_up_bf, ffn_conv_w[l], ffn_conv_b[l], w_down_bf,
                       (ln2_g[l], ln2_b[l]), groups=bsz, tiles=seq // tm, tm=tm, pad=SUBLANE,
                       tshift=1, state=None, alpha=alpha)
    y_prompt = y_p.reshape(bsz, seq, d)
    k_prompt = kt.reshape(bsz, n_kh, HEAD_DIM, seq).transpose(0, 3, 1, 2)[None]
    v_prompt = v.reshape(1, bsz, seq, width // HEAD_LANES, HEAD_LANES)
    conv_prompt = u.reshape(bsz, seq, -1)[None, :, seq - (conv_taps - 1):, :]
    ffn_prompt = tail_p[None, :, SUBLANE - (ffn_taps - 1):, :]

    xs_tm = x_sample.swapaxes(0, 1).reshape(n_rows_s, d)
    qs, kts, _, vs, _, us, ks = _in_projection(xs_tm, mod_s[None], ln_emb_g, ln_emb_b, w_in_bf,
                                               groups=n_new, tm=n_seq, emit_k_rows=True)
    to_sm = lambda a: a.reshape(n_new, n_seq, -1).swapaxes(0, 1)
    pad8 = lambda a: jnp.pad(to_sm(a), ((0, 0), (0, SUBLANE - n_new), (0, 0)))
    cache_kt = cache_k[l].transpose(0, 2, 3, 1).reshape(n_phys, width, page)
    cache_vr = cache_v[l].reshape(n_phys, page * (width // HEAD_LANES), HEAD_LANES)
    attn_s8 = _sample_attention(pad8(qs), pad8(ks), pad8(vs), cache_kt, cache_vr,
                                page_table, lam_params, subln_g[l], n_new, lambda_init)
    attn_s = attn_s8[:, :n_new, :].astype(BF16).swapaxes(0, 1).reshape(n_rows_s, width)
    full_tm = jnp.concatenate([state_conv[l].swapaxes(0, 1), us.reshape(n_new, n_seq, -1)], axis=0)
    conv_s = _sample_conv(full_tm, conv_w[l], conv_b[l], conv_ln_g[l], conv_ln_b[l],
                          n_new).reshape(n_rows_s, -1)
    state_tm = state_ffn[l].swapaxes(0, 1).reshape((ffn_taps - 1) * n_seq, 2 * d_ff)
    mod_rows_tm = jnp.tile(mod_s, (n_new, 1))[None]
    y_s, tail_s = _ffn(xs_tm, attn_s, conv_s, mod_rows_tm,
                       (ln_emb_g, ln_emb_b), w_out_bf, (ln1_g[l], ln1_b[l]), w_up_bf,
                       ffn_conv_w[l], ffn_conv_b[l], w_down_bf, (ln2_g[l], ln2_b[l]),
                       groups=1, tiles=1, tm=n_rows_s, pad=(ffn_taps - 1) * n_seq,
                       tshift=n_seq, state=state_tm, alpha=alpha)
    y_sample = y_s.reshape(n_new, n_seq, d).swapaxes(0, 1)
    k_sample = kts.reshape(n_new, n_kh, HEAD_DIM, n_seq).transpose(3, 0, 1, 2)[None]
    v_sample = to_sm(vs).reshape(1, n_seq, n_new, width // HEAD_LANES, HEAD_LANES)
    conv_sample = full_tm[n_new:].swapaxes(0, 1)[None]
    ffn_sample = tail_s.reshape(ffn_taps - 1, n_seq, 2 * d_ff).swapaxes(0, 1)[None]

    return (y_prompt, y_sample, k_prompt, v_prompt, conv_prompt, ffn_prompt,
            k_sample, v_sample, conv_sample, ffn_sample)
```

```python
import functools
import math

import jax
import jax.numpy as jnp
from jax import lax
from jax.experimental import pallas as pl
from jax.experimental.pallas import tpu as pltpu

F32 = jnp.float32
BF16 = jnp.bfloat16

LN_EPS = 1e-5
HEAD_DIM = 64
HEAD_LANES = 2 * HEAD_DIM
LANE = 128
SUBLANE = 8
VMEM_LIMIT = 56 * 1024 * 1024
NEG_BIG = -1e30
ALIBI_POS_SPLIT = 64


def _ln(x, g, b):
    mu = jnp.mean(x, axis=-1, keepdims=True)
    xc = x - mu
    var = jnp.mean(xc * xc, axis=-1, keepdims=True)
    return xc * lax.rsqrt(var + LN_EPS) * g + b


def _silu(x):
    return x * jax.nn.sigmoid(x)


def _dot(a, b):
    return jnp.dot(a, b, preferred_element_type=F32)


def _const_spec(shape):
    nd = len(shape)
    return pl.BlockSpec(shape, lambda *_: (0,) * nd, pipeline_mode=pl.Buffered(1))


def _alibi_slope(h, n_heads):
    return 2.0 ** (-8.0 * (h + 1) / n_heads)


def _mod_kernel(c_ref, w_ref, b_ref, o_ref):
    a = _silu(c_ref[...])
    o_ref[...] = jnp.dot(a, w_ref[...], precision=lax.Precision.HIGHEST,
                         preferred_element_type=F32) + b_ref[...]


def _modulation(c_all, w_ada, b_ada):
    rows, d = c_all.shape
    n_out = w_ada.shape[1]
    return pl.pallas_call(
        _mod_kernel,
        grid=(n_out // d,),
        in_specs=[pl.BlockSpec((rows, d), lambda j: (0, 0)),
                  pl.BlockSpec((d, d), lambda j: (0, j)),
                  pl.BlockSpec((1, d), lambda j: (0, j))],
        out_specs=pl.BlockSpec((rows, d), lambda j: (0, j)),
        out_shape=jax.ShapeDtypeStruct((rows, n_out), F32),
        compiler_params=pltpu.CompilerParams(vmem_limit_bytes=VMEM_LIMIT),
        name="modulation",
    )(c_all, w_ada, b_ada.reshape(1, n_out))


def _inproj_kernel(x_ref, sh_ref, sc_ref, lg_ref, lb_ref, w_ref, *out_refs, width, names):
    out = dict(zip(names, out_refs))
    tm = x_ref.shape[0]
    xp = _ln(x_ref[...], lg_ref[...], lb_ref[...])
    h = (xp * (1.0 + sc_ref[...]) + sh_ref[...]).astype(BF16)

    def proj(i):
        return _dot(h, w_ref[:, i * width:(i + 1) * width])

    def emit(name, value):
        if name in out:
            out[name][...] = value.astype(out[name].dtype)

    q = proj(0) * (HEAD_DIM ** -0.5)
    emit("q", q)
    if "qt_b" in out:
        emit("qt_b", q.T)
    k = proj(1)
    emit("k", k)
    emit("k_b", k)
    emit("kt", k.T)
    v = proj(2)
    emit("v", v)
    if "vt_b" in out:
        emit("vt_b", v.T)
    if "v_heads" in out:
        n_heads = width // HEAD_LANES
        for hd in range(n_heads):
            out["v_heads"][pl.ds(hd, tm, stride=n_heads), :] = (
                v[:, hd * HEAD_LANES:(hd + 1) * HEAD_LANES])
    emit("u", proj(3) * jax.nn.sigmoid(proj(4)))


def _in_projection(x2d, mod, ln_g, ln_b, w_in_bf, *, groups, tm, names):
    n, d = x2d.shape
    width = w_in_bf.shape[1] // 5
    n_heads = width // HEAD_LANES
    per_group = n // groups // tm
    mod_groups, r = mod.shape[0], mod.shape[1]

    def mod_spec(comp):
        return pl.BlockSpec((None, r, d),
                            lambda i: ((i // per_group) % mod_groups, 0, comp))

    row_spec = lambda rows, w: pl.BlockSpec((rows, w), lambda i: (i, 0))
    t_spec = pl.BlockSpec((None, width, tm), lambda i: (i // per_group, 0, i % per_group))
    t_shape = (groups, width, n // groups)
    dtypes = {"q": BF16, "k": F32, "v": F32, "u": F32, "k_b": BF16, "v_heads": F32,
              "kt": F32, "qt_b": BF16, "vt_b": BF16}
    out_specs, out_shape = [], []
    for nm in names:
        if nm in ("kt", "qt_b", "vt_b"):
            spec, shape = t_spec, t_shape
        elif nm == "v_heads":
            spec, shape = row_spec(tm * n_heads, HEAD_LANES), (n * n_heads, HEAD_LANES)
        else:
            spec, shape = row_spec(tm, width), (n, width)
        out_specs.append(spec)
        out_shape.append(jax.ShapeDtypeStruct(shape, dtypes[nm]))
    outs = pl.pallas_call(
        functools.partial(_inproj_kernel, width=width, names=tuple(names)),
        grid=(n // tm,),
        in_specs=[row_spec(tm, d), mod_spec(0), mod_spec(1),
                  _const_spec((1, d)), _const_spec((1, d)),
                  _const_spec(w_in_bf.shape)],
        out_specs=out_specs,
        out_shape=out_shape,
        compiler_params=pltpu.CompilerParams(
            dimension_semantics=("arbitrary",), vmem_limit_bytes=VMEM_LIMIT),
        name="in_projection",
    )(x2d, mod, mod, ln_g.reshape(1, d), ln_b.reshape(1, d), w_in_bf)
    return dict(zip(names, outs))


def _lambda_value(lq1_ref, lk1_ref, lq2_ref, lk2_ref, lambda_init):
    s1 = jnp.sum(lq1_ref[...] * lk1_ref[...], axis=-1, keepdims=True)
    s2 = jnp.sum(lq2_ref[...] * lk2_ref[...], axis=-1, keepdims=True)
    return jnp.exp(s1) - jnp.exp(s2) + lambda_init


def _sub_ln(o, g, lambda_init):
    ms = jnp.mean(o * o, axis=-1, keepdims=True)
    return o * lax.rsqrt(ms + LN_EPS) * g * (1.0 - lambda_init)


def _prompt_attn_kernel(qt_ref, k_ref, vt_ref, lq1_ref, lk1_ref, lq2_ref, lk2_ref, g_ref,
                        o_ref, qaug_scr, m_scr, l_scr, acc_scr, *, n_heads, tq, tk, lambda_init):
    qi = pl.program_id(1)
    ki = pl.program_id(2)

    @pl.when(ki == 0)
    def _init():
        m_scr[...] = jnp.full(m_scr.shape, NEG_BIG, F32)
        l_scr[...] = jnp.zeros(l_scr.shape, F32)
        acc_scr[...] = jnp.zeros(acc_scr.shape, F32)
        r = lax.broadcasted_iota(jnp.int32, (HEAD_LANES, 2 * tq), 0)
        c = lax.broadcasted_iota(jnp.int32, (HEAD_LANES, 2 * tq), 1)
        own_map = (r < HEAD_DIM) == (c < tq)
        for h in range(n_heads):
            qt = qt_ref[h * HEAD_LANES:(h + 1) * HEAD_LANES, :].astype(F32)
            top = jnp.where(own_map, jnp.concatenate([qt, qt], axis=1), 0.0)
            slope = _alibi_slope(h, n_heads)
            bot = jnp.where(r == 0, ALIBI_POS_SPLIT * slope, jnp.where(r == 1, slope, 0.0))
            qaug_scr[h] = jnp.concatenate([top, bot], axis=0).astype(BF16)

    def step(on_diagonal):
        kr = lax.broadcasted_iota(jnp.int32, (tk, LANE), 0)
        lane = lax.broadcasted_iota(jnp.int32, (tk, LANE), 1)
        kpos = ki * tk + kr
        pos_hi = kpos // ALIBI_POS_SPLIT
        pos_lo = kpos % ALIBI_POS_SPLIT
        pos_cols = jnp.where(lane == 0, pos_hi, jnp.where(lane == 1, pos_lo, 0)
                             ).astype(F32).astype(BF16)
        if on_diagonal:
            key = lax.broadcasted_iota(jnp.int32, (tk, 2 * tq), 0)
            qry = lax.broadcasted_iota(jnp.int32, (tk, 2 * tq), 1)
            visible = key <= jnp.where(qry >= tq, qry - tq, qry)
        for h in range(n_heads):
            cols = slice(h * HEAD_LANES, (h + 1) * HEAD_LANES)
            kaug = jnp.concatenate([k_ref[:, cols], pos_cols], axis=1)
            s = _dot(kaug, qaug_scr[h])
            if on_diagonal:
                s = jnp.where(visible, s, NEG_BIG)
            m_prev = m_scr[h]
            m_new = jnp.maximum(m_prev, jnp.max(s, axis=0, keepdims=True))
            alpha = jnp.exp(m_prev - m_new)
            p = jnp.exp(s - m_new)
            l_scr[h] = alpha * l_scr[h] + jnp.sum(p, axis=0, keepdims=True)
            acc_scr[h] = alpha * acc_scr[h] + _dot(vt_ref[cols, :], p.astype(BF16))
            m_scr[h] = m_new

    @pl.when(ki < qi)
    def _below_diagonal():
        step(False)

    @pl.when(ki == qi)
    def _diagonal_and_finalize():
        step(True)
        lam = _lambda_value(lq1_ref, lk1_ref, lq2_ref, lk2_ref, lambda_init)
        for h in range(n_heads):
            ot = acc_scr[h] * (1.0 / l_scr[h])
            o = ot[:, :tq] - lam * ot[:, tq:]
            ms = jnp.mean(o * o, axis=0, keepdims=True)
            o = o * lax.rsqrt(ms + LN_EPS) * g_ref[...] * (1.0 - lambda_init)
            o_ref[:, h * HEAD_LANES:(h + 1) * HEAD_LANES] = o.T.astype(o_ref.dtype)


def _prompt_attention(qt_b, k_b, vt_b, lam_params, subln_g, lambda_init, tq=512):
    bsz, width, seq = qt_b.shape
    n_heads = width // HEAD_LANES
    tk = tq
    nq = seq // tq
    assert (seq - 1) // ALIBI_POS_SPLIT < 256, "key position parts must be exact in bf16"
    lq1, lk1, lq2, lk2 = [p.reshape(1, HEAD_DIM) for p in lam_params]
    qt_spec = pl.BlockSpec((None, width, tq), lambda b, i, j: (b, 0, i))
    k_spec = pl.BlockSpec((tk, width), lambda b, i, j: (b * nq + jnp.minimum(i, j), 0))
    vt_spec = pl.BlockSpec((None, width, tk), lambda b, i, j: (b, 0, jnp.minimum(i, j)))
    o_spec = pl.BlockSpec((tq, width), lambda b, i, j: (b * nq + i, 0))
    small = lambda r, w: pl.BlockSpec((r, w), lambda b, i, j: (0, 0))
    return pl.pallas_call(
        functools.partial(_prompt_attn_kernel, n_heads=n_heads, tq=tq, tk=tk,
                          lambda_init=lambda_init),
        grid=(bsz, nq, nq),
        in_specs=[qt_spec, k_spec, vt_spec,
                  small(1, HEAD_DIM), small(1, HEAD_DIM), small(1, HEAD_DIM), small(1, HEAD_DIM),
                  small(HEAD_LANES, 1)],
        out_specs=o_spec,
        out_shape=jax.ShapeDtypeStruct((bsz * seq, width), BF16),
        scratch_shapes=[pltpu.VMEM((n_heads, 2 * HEAD_LANES, 2 * tq), BF16),
                        pltpu.VMEM((n_heads, 1, 2 * tq), F32),
                        pltpu.VMEM((n_heads, 1, 2 * tq), F32),
                        pltpu.VMEM((n_heads, HEAD_LANES, 2 * tq), F32)],
        compiler_params=pltpu.CompilerParams(
            dimension_semantics=("arbitrary", "arbitrary", "arbitrary"),
            vmem_limit_bytes=VMEM_LIMIT),
        name="prompt_attention",
    )(qt_b, k_b, vt_b, lq1, lk1, lq2, lk2, subln_g.reshape(HEAD_LANES, 1))


def _sample_attn_kernel(pt_ref, q_ref, kn_ref, vn_ref, lq1_ref, lk1_ref, lq2_ref, lk2_ref,
                        g_ref, *rest, n_pages, page, n_new, n_heads, lambda_init):
    kt_refs = rest[:n_pages]
    v_refs = rest[n_pages:2 * n_pages]
    o_ref = rest[2 * n_pages]
    s_scr = rest[2 * n_pages + 1]
    del pt_ref
    width = n_heads * HEAD_LANES
    n_maps = 2 * n_heads
    rows = n_new * n_maps
    past = n_pages * page

    q = q_ref[0].astype(F32)
    mp = lax.broadcasted_iota(jnp.int32, (n_maps, width), 0)
    cl = lax.broadcasted_iota(jnp.int32, (n_maps, width), 1)
    map_lanes = (cl // HEAD_DIM) == mp
    qbd32 = jnp.concatenate(
        [jnp.where(map_lanes, jnp.broadcast_to(q[i:i + 1, :], (n_maps, width)), 0.0)
         for i in range(n_new)], axis=0)
    qbd = qbd32.astype(BF16)

    for p in range(n_pages):
        s_scr[:, p * page:(p + 1) * page] = _dot(qbd, kt_refs[p][...].astype(BF16))

    rmap = lax.broadcasted_iota(jnp.int32, (rows, 1), 0) % n_maps
    rqry = lax.broadcasted_iota(jnp.int32, (rows, 1), 0) // n_maps
    slope = jnp.zeros((rows, 1), F32)
    for h in range(n_heads):
        slope = jnp.where(rmap // 2 == h, _alibi_slope(h, n_heads), slope)
    tpos = lax.broadcasted_iota(jnp.int32, (1, past), 1).astype(F32)
    s = s_scr[...] + slope * tpos

    kn = kn_ref[0]
    vn = vn_ref[0]
    s_new = []
    for j in range(n_new):
        sj = jnp.sum(qbd32 * kn[j:j + 1, :], axis=-1, keepdims=True) + slope * float(past + j)
        s_new.append(jnp.where(rqry >= j, sj, NEG_BIG))
    m = jnp.max(s, axis=-1, keepdims=True)
    for sj in s_new:
        m = jnp.maximum(m, sj)
    pr = jnp.exp(s - m)
    l = jnp.sum(pr, axis=-1, keepdims=True)
    prb = pr.astype(BF16)
    p_new = []
    for sj in s_new:
        pj = jnp.exp(sj - m)
        l = l + pj
        p_new.append(pj)
    inv_l = 1.0 / l

    lam = _lambda_value(lq1_ref, lk1_ref, lq2_ref, lk2_ref, lambda_init)
    mrow = lax.broadcasted_iota(jnp.int32, (n_maps, 1), 0)
    for h in range(n_heads):
        cols = slice(h * HEAD_LANES, (h + 1) * HEAD_LANES)
        out = jnp.zeros((rows, HEAD_LANES), F32)
        for j, pj in enumerate(p_new):
            out = out + pj * vn[j:j + 1, cols]
        for p in range(n_pages):
            vh = v_refs[p][pl.ds(h, page, stride=n_heads), :].astype(BF16)
            out = out + _dot(prb[:, p * page:(p + 1) * page], vh)
        out = out * inv_l
        weight = jnp.where(mrow == 2 * h, 1.0, jnp.where(mrow == 2 * h + 1, -lam, 0.0))
        for i in range(n_new):
            o = jnp.sum(out[i * n_maps:(i + 1) * n_maps, :] * weight, axis=0, keepdims=True)
            o_ref[0, i:i + 1, cols] = _sub_ln(o, g_ref[...], lambda_init)
    o_ref[0, n_new:, :] = jnp.zeros((SUBLANE - n_new, width), F32)


def _sample_attention(q8, kn8, vn8, cache_kt, cache_v, page_table, lam_params, subln_g,
                      n_new, lambda_init):
    n_seq, _, width = q8.shape
    n_pages = page_table.shape[1]
    page = cache_kt.shape[2]
    n_heads = width // HEAD_LANES
    assert 2 * n_heads == SUBLANE and n_new <= SUBLANE
    lq1, lk1, lq2, lk2 = [p.reshape(1, HEAD_DIM) for p in lam_params]
    seq_spec = pl.BlockSpec((1, SUBLANE, width), lambda s, pt: (s, 0, 0))
    small = lambda w: pl.BlockSpec((1, w), lambda s, pt: (0, 0))

    def page_spec(p, shape):
        return pl.BlockSpec((None,) + shape, lambda s, pt: (pt[s * n_pages + p], 0, 0))

    grid_spec = pltpu.PrefetchScalarGridSpec(
        num_scalar_prefetch=1,
        grid=(n_seq,),
        in_specs=[seq_spec, seq_spec, seq_spec,
                  small(HEAD_DIM), small(HEAD_DIM), small(HEAD_DIM), small(HEAD_DIM),
                  small(HEAD_LANES)]
                 + [page_spec(p, cache_kt.shape[1:]) for p in range(n_pages)]
                 + [page_spec(p, cache_v.shape[1:]) for p in range(n_pages)],
        out_specs=seq_spec,
        scratch_shapes=[pltpu.VMEM((n_new * 2 * n_heads, n_pages * page), F32)],
    )
    return pl.pallas_call(
        functools.partial(_sample_attn_kernel, n_pages=n_pages, page=page, n_new=n_new,
                          n_heads=n_heads, lambda_init=lambda_init),
        grid_spec=grid_spec,
        out_shape=jax.ShapeDtypeStruct((n_seq, SUBLANE, width), F32),
        compiler_params=pltpu.CompilerParams(
            dimension_semantics=("arbitrary",), vmem_limit_bytes=VMEM_LIMIT),
        name="sample_attention",
    )(page_table.reshape(-1), q8, kn8, vn8, lq1, lk1, lq2, lk2,
      subln_g.reshape(1, HEAD_LANES), *([cache_kt] * n_pages), *([cache_v] * n_pages))


def _prompt_conv_kernel(u_ref, w_ref, b_ref, lg_ref, lb_ref, o_ref, buf, wrep, cv_scr, *,
                        tt, taps, halo, rc):
    t = pl.program_id(1)
    n_slab = buf.shape[0]

    @pl.when(t == 0)
    def _start_of_sequence():
        buf[:, 0:halo, :] = jnp.zeros((n_slab, halo, LANE), F32)
        for s in range(n_slab):
            cols = slice(s * LANE, (s + 1) * LANE)
            wrep[taps, s] = jnp.broadcast_to(b_ref[:, cols], (SUBLANE, LANE))
            for j in range(taps):
                wrep[j, s] = jnp.broadcast_to(w_ref[j:j + 1, cols], (SUBLANE, LANE))

    for s in range(n_slab):
        buf[s, halo:halo + tt, :] = u_ref[:, s * LANE:(s + 1) * LANE]
    base = halo - (taps - 1)
    groups = rc // SUBLANE

    def block_conv(i, carry):
        s = i % n_slab
        r0 = pl.multiple_of((i // n_slab) * rc, rc)
        acc = jnp.broadcast_to(wrep[taps, s][None], (groups, SUBLANE, LANE))
        for phase in range(SUBLANE):
            phase_taps = range(phase, taps, SUBLANE)
            n_groups = groups + len(phase_taps) - 1
            x = buf[s, pl.ds(base + r0 + phase, n_groups * SUBLANE), :].reshape(
                n_groups, SUBLANE, LANE)
            for a, j in enumerate(phase_taps):
                acc = acc + wrep[j, s][None] * x[a:a + groups]
        cv_scr[s, pl.ds(r0, rc), :] = acc.reshape(rc, LANE)
        return carry

    lax.fori_loop(0, (tt // rc) * n_slab, block_conv, 0)
    cv = jnp.concatenate([cv_scr[s] for s in range(n_slab)], axis=1)
    o_ref[...] = _silu(_ln(cv, lg_ref[...], lb_ref[...])).astype(o_ref.dtype)
    for s in range(n_slab):
        buf[s, 0:halo, :] = buf[s, tt:tt + halo, :]


def _prompt_conv(u, conv_w, conv_b, ln_g, ln_b, bsz, seq, tt=512, rc=128):
    n, width = u.shape
    taps = conv_w.shape[0]
    halo = -(-(taps - 1) // SUBLANE) * SUBLANE
    nt = seq // tt
    row_spec = pl.BlockSpec((tt, width), lambda b, t: (b * nt + t, 0))
    const = lambda r: pl.BlockSpec((r, width), lambda b, t: (0, 0))
    return pl.pallas_call(
        functools.partial(_prompt_conv_kernel, tt=tt, taps=taps, halo=halo, rc=rc),
        grid=(bsz, nt),
        in_specs=[row_spec, const(taps), const(1), const(1), const(1)],
        out_specs=row_spec,
        out_shape=jax.ShapeDtypeStruct((n, width), BF16),
        scratch_shapes=[pltpu.VMEM((width // LANE, halo + tt, LANE), F32),
                        pltpu.VMEM((taps + 1, width // LANE, SUBLANE, LANE), F32),
                        pltpu.VMEM((width // LANE, tt, LANE), F32)],
        compiler_params=pltpu.CompilerParams(
            dimension_semantics=("arbitrary", "arbitrary"), vmem_limit_bytes=VMEM_LIMIT),
        name="prompt_conv",
    )(u, conv_w, conv_b.reshape(1, width), ln_g.reshape(1, width), ln_b.reshape(1, width))


def _sample_conv_kernel(full_ref, w_ref, b_ref, lg_ref, lb_ref, o_ref, *, n_new, n_seq, taps):
    for t in range(n_new):
        cv = jnp.broadcast_to(b_ref[...], (n_seq, full_ref.shape[2]))
        for j in range(taps):
            cv = cv + w_ref[j:j + 1, :] * full_ref[t + j]
        o_ref[t] = _silu(_ln(cv, lg_ref[...], lb_ref[...])).astype(o_ref.dtype)


def _sample_conv(full_tm, conv_w, conv_b, ln_g, ln_b, n_new):
    n_time, n_seq, width = full_tm.shape
    taps = conv_w.shape[0]
    return pl.pallas_call(
        functools.partial(_sample_conv_kernel, n_new=n_new, n_seq=n_seq, taps=taps),
        out_shape=jax.ShapeDtypeStruct((n_new, n_seq, width), BF16),
        compiler_params=pltpu.CompilerParams(vmem_limit_bytes=VMEM_LIMIT),
        name="sample_conv",
    )(full_tm, conv_w, conv_b.reshape(1, width), ln_g.reshape(1, width), ln_b.reshape(1, width))


def _ffn_kernel(*refs, has_state, tm, pad, tshift, fc, d_ff, alpha):
    (x_ref, attn_ref, conv_ref, g1_ref, sh2_ref, sc2_ref, g2_ref, leg_ref, leb_ref,
     wout_ref, l1g_ref, l1b_ref, wup_ref, cw_ref, cb_ref, wdn_ref, l2g_ref, l2b_ref) = refs[:18]
    refs = refs[18:]
    if has_state:
        state_ref, refs = refs[0], refs[1:]
    y_ref, tail_ref, ubuf, acc_ref = refs
    half = attn_ref.shape[1]
    n_slab = fc // LANE

    if has_state:
        hist_ref = state_ref
    else:
        hist_ref = tail_ref

        @pl.when(pl.program_id(1) == 0)
        def _zero_history():
            tail_ref[...] = jnp.zeros(tail_ref.shape, F32)

    xp = _ln(x_ref[...], leg_ref[...], leb_ref[...])
    mix = _dot(attn_ref[...], wout_ref[0:half, :]) + _dot(conv_ref[...], wout_ref[half:, :])
    x1 = _ln(alpha * xp + g1_ref[...] * mix, l1g_ref[...], l1b_ref[...])
    h2 = (x1 * (1.0 + sc2_ref[...]) + sh2_ref[...]).astype(BF16)

    n_chunks = d_ff // fc

    def up_project(c):
        for part in range(2):
            col = part * d_ff + c * fc
            up = _dot(h2, wup_ref[:, col:col + fc])
            for s in range(n_slab):
                cols = slice(col + s * LANE, col + (s + 1) * LANE)
                slab = ((c % 2) * 2 + part) * n_slab + s
                ubuf[slab, 0:pad, :] = hist_ref[:, cols]
                ubuf[slab, pad:pad + tm, :] = up[:, s * LANE:(s + 1) * LANE]
            tail_ref[:, col:col + fc] = up[tm - pad:, :]

    def conv_half(c, part):
        parts = []
        for s in range(n_slab):
            col = part * d_ff + c * fc + s * LANE
            cols = slice(col, col + LANE)
            slab = ((c % 2) * 2 + part) * n_slab + s
            parts.append(cw_ref[0:1, cols] * ubuf[slab, pad - 2 * tshift:pad - 2 * tshift + tm, :]
                         + cw_ref[1:2, cols] * ubuf[slab, pad - tshift:pad - tshift + tm, :]
                         + cw_ref[2:3, cols] * ubuf[slab, pad:pad + tm, :] + cb_ref[:, cols])
        return jnp.concatenate(parts, axis=1)

    up_project(0)
    for c in range(n_chunks):
        if c + 1 < n_chunks:
            up_project(c + 1)
        g = (_silu(conv_half(c, 0)) * conv_half(c, 1)).astype(BF16)
        contrib = _dot(g, wdn_ref[c * fc:(c + 1) * fc, :])
        if c == 0:
            acc_ref[...] = contrib
        else:
            acc_ref[...] += contrib

    y_ref[...] = _ln(alpha * x1 + g2_ref[...] * acc_ref[...], l2g_ref[...], l2b_ref[...])


def _ffn(x2d, attn, conv, mod, ln_emb, w_out_bf, ln1, w_up_bf, ffn_conv_w, ffn_conv_b,
         w_down_bf, ln2, *, groups, tiles, tm, pad, tshift, state, alpha, fc=256):
    n, d = x2d.shape
    half = attn.shape[1]
    d_ff = w_down_bf.shape[0]
    r = mod.shape[1]
    has_state = state is not None
    assert pad == 2 * tshift or not has_state
    assert tm >= pad
    row = lambda w: pl.BlockSpec((tm, w), lambda b, t: (b * tiles + t, 0))
    mod_spec = lambda comp: pl.BlockSpec((None, r, d), lambda b, t: (b, 0, comp))
    vec = lambda a: a.reshape(1, -1)
    args = [x2d, attn, conv, mod, mod, mod, mod, vec(ln_emb[0]), vec(ln_emb[1]),
            w_out_bf, vec(ln1[0]), vec(ln1[1]), w_up_bf, ffn_conv_w, vec(ffn_conv_b),
            w_down_bf, vec(ln2[0]), vec(ln2[1])]
    in_specs = [row(d), row(half), row(half), mod_spec(2), mod_spec(3), mod_spec(4), mod_spec(5),
                _const_spec((1, d)), _const_spec((1, d)),
                _const_spec(w_out_bf.shape), _const_spec((1, d)), _const_spec((1, d)),
                _const_spec(w_up_bf.shape), _const_spec(ffn_conv_w.shape),
                _const_spec((1, 2 * d_ff)), _const_spec(w_down_bf.shape),
                _const_spec((1, d)), _const_spec((1, d))]
    if has_state:
        args.append(state)
        in_specs.append(_const_spec(state.shape))
    y, tail = pl.pallas_call(
        functools.partial(_ffn_kernel, has_state=has_state, tm=tm, pad=pad,
                          tshift=tshift, fc=fc, d_ff=d_ff, alpha=alpha),
        grid=(groups, tiles),
        in_specs=in_specs,
        out_specs=[row(d), pl.BlockSpec((None, pad, 2 * d_ff), lambda b, t: (b, 0, 0))],
        out_shape=[jax.ShapeDtypeStruct((n, d), F32),
                   jax.ShapeDtypeStruct((groups, pad, 2 * d_ff), F32)],
        scratch_shapes=[pltpu.VMEM((4 * (fc // LANE), pad + tm, LANE), F32),
                        pltpu.VMEM((tm, d), F32)],
        compiler_params=pltpu.CompilerParams(
            dimension_semantics=("arbitrary", "arbitrary"), vmem_limit_bytes=VMEM_LIMIT),
        name="ffn",
    )(*args)
    return y, tail


def kernel(x_prompt, x_sample, c_prompt, c_sample, cache_k, cache_v, page_table, state_conv, state_ffn, ln_emb_g, ln_emb_b, w_ada, b_ada, w_in, lambda_q1, lambda_k1, lambda_q2, lambda_k2, subln_g, conv_w, conv_b, conv_ln_g, conv_ln_b, w_out, ln1_g, ln1_b, w_up, ffn_conv_w, ffn_conv_b, w_down, ln2_g, ln2_b):
    bsz, seq, d = x_prompt.shape
    n_seq, n_new, _ = x_sample.shape
    depth = w_ada.shape[0]
    assert depth == 1, "the prompt/sample activations are threaded for a single layer"
    n_phys, page = cache_k.shape[1], cache_k.shape[2]
    width = cache_k.shape[3] * cache_k.shape[4]
    d_ff = w_down.shape[1]
    conv_taps = conv_w.shape[1]
    ffn_taps = ffn_conv_w.shape[1]
    assert ffn_taps == 3
    alpha = (2 * depth) ** 0.25
    tm = 512
    n_rows_s = n_seq * n_new
    n_kh = width // HEAD_DIM
    n_vh = width // HEAD_LANES

    l = 0
    lambda_init = 0.8 - 0.6 * math.exp(-0.3 * l)
    lam_params = (lambda_q1[l], lambda_k1[l], lambda_q2[l], lambda_k2[l])
    w_in_bf = w_in[l].astype(BF16)
    w_out_bf = w_out[l].astype(BF16)
    w_up_bf = w_up[l].astype(BF16)
    w_down_bf = w_down[l].astype(BF16)

    mod = _modulation(jnp.concatenate([c_prompt, c_sample], axis=0), w_ada[l], b_ada[l])
    mod_p = mod[:bsz].reshape(bsz, 1, 6 * d)
    mod_s = mod[bsz:]

    xp2d = x_prompt.reshape(bsz * seq, d)
    pp = _in_projection(xp2d, mod_p, ln_emb_g, ln_emb_b, w_in_bf, groups=bsz, tm=tm,
                        names=("qt_b", "k_b", "kt", "v_heads", "vt_b", "u"))
    u = pp["u"]
    attn = _prompt_attention(pp["qt_b"], pp["k_b"], pp["vt_b"], lam_params, subln_g[l],
                             lambda_init)
    conv = _prompt_conv(u, conv_w[l], conv_b[l], conv_ln_g[l], conv_ln_b[l], bsz, seq)
    y_p, tail_p = _ffn(xp2d, attn, conv, mod_p, (ln_emb_g, ln_emb_b), w_out_bf,
                       (ln1_g[l], ln1_b[l]), w_up_bf, ffn_conv_w[l], ffn_conv_b[l], w_down_bf,
                       (ln2_g[l], ln2_b[l]), groups=bsz, tiles=seq // tm, tm=tm,
                       pad=SUBLANE, tshift=1, state=None, alpha=alpha)
    y_prompt = y_p.reshape(bsz, seq, d)
    k_prompt = pp["kt"].reshape(bsz, n_kh, HEAD_DIM, seq).transpose(0, 3, 1, 2)[None]
    v_prompt = pp["v_heads"].reshape(1, bsz, seq, n_vh, HEAD_LANES)
    conv_prompt = u.reshape(bsz, seq, -1)[None, :, seq - (conv_taps - 1):, :]
    ffn_prompt = tail_p[None, :, SUBLANE - (ffn_taps - 1):, :]

    xs_tm = x_sample.swapaxes(0, 1).reshape(n_rows_s, d)
    ps = _in_projection(xs_tm, mod_s[None], ln_emb_g, ln_emb_b, w_in_bf, groups=n_new, tm=n_seq,
                        names=("q", "k", "kt", "v", "u"))
    qs, ks, kts, vs, us = ps["q"], ps["k"], ps["kt"], ps["v"], ps["u"]
    to_sm = lambda a: a.reshape(n_new, n_seq, -1).swapaxes(0, 1)
    pad8 = lambda a: jnp.pad(to_sm(a), ((0, 0), (0, SUBLANE - n_new), (0, 0)))
    cache_kt = cache_k[l].transpose(0, 2, 3, 1).reshape(n_phys, width, page)
    cache_vr = cache_v[l].reshape(n_phys, page * n_vh, HEAD_LANES)
    attn_s8 = _sample_attention(pad8(qs), pad8(ks), pad8(vs), cache_kt, cache_vr,
                                page_table, lam_params, subln_g[l], n_new, lambda_init)
    attn_s = attn_s8[:, :n_new, :].astype(BF16).swapaxes(0, 1).reshape(n_rows_s, width)
    full_tm = jnp.concatenate([state_conv[l].swapaxes(0, 1), us.reshape(n_new, n_seq, -1)], axis=0)
    conv_s = _sample_conv(full_tm, conv_w[l], conv_b[l], conv_ln_g[l], conv_ln_b[l],
                          n_new).reshape(n_rows_s, -1)
    state_tm = state_ffn[l].swapaxes(0, 1).reshape((ffn_taps - 1) * n_seq, 2 * d_ff)
    mod_rows_tm = jnp.tile(mod_s, (n_new, 1))[None]
    y_s, tail_s = _ffn(xs_tm, attn_s, conv_s, mod_rows_tm,
                       (ln_emb_g, ln_emb_b), w_out_bf, (ln1_g[l], ln1_b[l]), w_up_bf,
                       ffn_conv_w[l], ffn_conv_b[l], w_down_bf, (ln2_g[l], ln2_b[l]),
                       groups=1, tiles=1, tm=n_rows_s, pad=(ffn_taps - 1) * n_seq,
                       tshift=n_seq, state=state_tm, alpha=alpha)
    y_sample = y_s.reshape(n_new, n_seq, d).swapaxes(0, 1)
    k_sample = kts.reshape(n_new, n_kh, HEAD_DIM, n_seq).transpose(3, 0, 1, 2)[None]
    v_sample = to_sm(vs).reshape(1, n_seq, n_new, n_vh, HEAD_LANES)
    conv_sample = full_tm[n_new:].swapaxes(0, 1)[None]
    ffn_sample = tail_s.reshape(ffn_taps - 1, n_seq, 2 * d_ff).swapaxes(0, 1)[None]

    return (y_prompt, y_sample, k_prompt, v_prompt, conv_prompt, ffn_prompt,
            k_sample, v_sample, conv_sample, ffn_sample)
```

```python
import functools
import math

import jax
import jax.numpy as jnp
from jax import lax
from jax.experimental import pallas as pl
from jax.experimental.pallas import tpu as pltpu

F32 = jnp.float32
BF16 = jnp.bfloat16

LN_EPS = 1e-5
HEAD_DIM = 64
HEAD_LANES = 2 * HEAD_DIM
LANE = 128
SUBLANE = 8
VMEM_LIMIT = 56 * 1024 * 1024
NEG_BIG = -1e30
ALIBI_POS_SPLIT = 64


def _ln(x, g, b):
    mu = jnp.mean(x, axis=-1, keepdims=True)
    xc = x - mu
    var = jnp.mean(xc * xc, axis=-1, keepdims=True)
    return xc * lax.rsqrt(var + LN_EPS) * g + b


def _silu(x):
    return x * jax.nn.sigmoid(x)


def _dot(a, b):
    return jnp.dot(a, b, preferred_element_type=F32)


def _const_spec(shape):
    nd = len(shape)
    return pl.BlockSpec(shape, lambda *_: (0,) * nd, pipeline_mode=pl.Buffered(1))


def _alibi_slope(h, n_heads):
    return 2.0 ** (-8.0 * (h + 1) / n_heads)


def _split_bf16(x):
    hi = x.astype(BF16)
    return hi, (x - hi.astype(F32)).astype(BF16)


def _mod_kernel(c_ref, w_ref, b_ref, o_ref):
    a_hi, a_lo = _split_bf16(_silu(c_ref[...]))
    w_hi, w_lo = _split_bf16(w_ref[...])
    o_ref[...] = _dot(a_hi, w_hi) + (_dot(a_hi, w_lo) + _dot(a_lo, w_hi)) + b_ref[...]


def _modulation(c_all, w_ada, b_ada):
    rows, d = c_all.shape
    n_out = w_ada.shape[1]
    return pl.pallas_call(
        _mod_kernel,
        grid=(n_out // d,),
        in_specs=[pl.BlockSpec((rows, d), lambda j: (0, 0)),
                  pl.BlockSpec((d, d), lambda j: (0, j)),
                  pl.BlockSpec((1, d), lambda j: (0, j))],
        out_specs=pl.BlockSpec((rows, d), lambda j: (0, j)),
        out_shape=jax.ShapeDtypeStruct((rows, n_out), F32),
        compiler_params=pltpu.CompilerParams(vmem_limit_bytes=VMEM_LIMIT),
        name="modulation",
    )(c_all, w_ada, b_ada.reshape(1, n_out))


def _inproj_kernel(x_ref, sh_ref, sc_ref, lg_ref, lb_ref, w_ref, *out_refs, width, names):
    out = dict(zip(names, out_refs))
    tm = x_ref.shape[0]
    xp = _ln(x_ref[...], lg_ref[...], lb_ref[...])
    h = (xp * (1.0 + sc_ref[...]) + sh_ref[...]).astype(BF16)

    def proj(i):
        return _dot(h, w_ref[:, i * width:(i + 1) * width])

    def emit(name, value):
        if name in out:
            out[name][...] = value.astype(out[name].dtype)

    emit("xp", xp)
    q = proj(0) * (HEAD_DIM ** -0.5)
    emit("q", q)
    if "qt_b" in out:
        emit("qt_b", q.T)
    k = proj(1)
    emit("k", k)
    emit("k_b", k)
    emit("kt", k.T)
    v = proj(2)
    emit("v", v)
    if "vt_b" in out:
        emit("vt_b", v.T)
    if "v_heads" in out:
        n_heads = width // HEAD_LANES
        for hd in range(n_heads):
            out["v_heads"][pl.ds(hd, tm, stride=n_heads), :] = (
                v[:, hd * HEAD_LANES:(hd + 1) * HEAD_LANES])
    emit("u", proj(3) * jax.nn.sigmoid(proj(4)))


def _in_projection(x2d, mod, ln_g, ln_b, w_in_bf, *, groups, tm, names):
    n, d = x2d.shape
    width = w_in_bf.shape[1] // 5
    n_heads = width // HEAD_LANES
    per_group = n // groups // tm
    mod_groups, r = mod.shape[0], mod.shape[1]

    def mod_spec(comp):
        return pl.BlockSpec((None, r, d),
                            lambda i: ((i // per_group) % mod_groups, 0, comp))

    row_spec = lambda rows, w: pl.BlockSpec((rows, w), lambda i: (i, 0))
    t_spec = pl.BlockSpec((None, width, tm), lambda i: (i // per_group, 0, i % per_group))
    t_shape = (groups, width, n // groups)
    dtypes = {"xp": F32, "q": BF16, "k": F32, "v": F32, "u": F32, "k_b": BF16, "v_heads": F32,
              "kt": F32, "qt_b": BF16, "vt_b": BF16}
    out_specs, out_shape = [], []
    for nm in names:
        if nm in ("kt", "qt_b", "vt_b"):
            spec, shape = t_spec, t_shape
        elif nm == "v_heads":
            spec, shape = row_spec(tm * n_heads, HEAD_LANES), (n * n_heads, HEAD_LANES)
        elif nm == "xp":
            spec, shape = row_spec(tm, d), (n, d)
        else:
            spec, shape = row_spec(tm, width), (n, width)
        out_specs.append(spec)
        out_shape.append(jax.ShapeDtypeStruct(shape, dtypes[nm]))
    outs = pl.pallas_call(
        functools.partial(_inproj_kernel, width=width, names=tuple(names)),
        grid=(n // tm,),
        in_specs=[row_spec(tm, d), mod_spec(0), mod_spec(1),
                  _const_spec((1, d)), _const_spec((1, d)),
                  _const_spec(w_in_bf.shape)],
        out_specs=out_specs,
        out_shape=out_shape,
        compiler_params=pltpu.CompilerParams(
            dimension_semantics=("arbitrary",), vmem_limit_bytes=VMEM_LIMIT),
        name="in_projection",
    )(x2d, mod, mod, ln_g.reshape(1, d), ln_b.reshape(1, d), w_in_bf)
    return dict(zip(names, outs))


def _lambda_value(lq1_ref, lk1_ref, lq2_ref, lk2_ref, lambda_init):
    s1 = jnp.sum(lq1_ref[...] * lk1_ref[...], axis=-1, keepdims=True)
    s2 = jnp.sum(lq2_ref[...] * lk2_ref[...], axis=-1, keepdims=True)
    return jnp.exp(s1) - jnp.exp(s2) + lambda_init


def _sub_ln(o, g, lambda_init):
    ms = jnp.mean(o * o, axis=-1, keepdims=True)
    return o * lax.rsqrt(ms + LN_EPS) * g * (1.0 - lambda_init)


def _prompt_attn_kernel(qi_tab, ki_tab, qt_ref, k_ref, vt_ref, lq1_ref, lk1_ref, lq2_ref, lk2_ref,
                        g_ref, o_ref, qaug_scr, pos_scr, s_scr, m_scr, l_scr, acc_scr, *,
                        n_heads, tq, tk, lambda_init):
    pair = pl.program_id(1)
    qi = qi_tab[pair]
    ki = ki_tab[pair]

    @pl.when(ki == 0)
    def _init():
        m_scr[...] = jnp.full(m_scr.shape, NEG_BIG, F32)
        l_scr[...] = jnp.zeros(l_scr.shape, F32)
        acc_scr[...] = jnp.zeros(acc_scr.shape, F32)
        r = lax.broadcasted_iota(jnp.int32, (HEAD_LANES, tq), 0)
        for h in range(n_heads):
            qt = qt_ref[h * HEAD_LANES:(h + 1) * HEAD_LANES, :].astype(F32)
            slope = _alibi_slope(h, n_heads)
            bot = jnp.where(r == 0, ALIBI_POS_SPLIT * slope, jnp.where(r == 1, slope, 0.0))
            for j in range(2):
                own = (r < HEAD_DIM) if j == 0 else (r >= HEAD_DIM)
                top = jnp.where(own, qt, 0.0)
                qaug_scr[2 * h + j] = jnp.concatenate([top, bot], axis=0).astype(BF16)

    def step(on_diagonal):
        kr = lax.broadcasted_iota(jnp.int32, (tk, LANE), 0)
        lane = lax.broadcasted_iota(jnp.int32, (tk, LANE), 1)
        kpos = ki * tk + kr
        pos_hi = kpos // ALIBI_POS_SPLIT
        pos_lo = kpos % ALIBI_POS_SPLIT
        pos_scr[...] = jnp.where(lane == 0, pos_hi, jnp.where(lane == 1, pos_lo, 0)
                                 ).astype(F32).astype(BF16)
        if on_diagonal:
            key = lax.broadcasted_iota(jnp.int32, (tk, tq), 0)
            qry = lax.broadcasted_iota(jnp.int32, (tk, tq), 1)
            visible = key <= qry

        def head_cols(slab):
            return slice((slab // 2) * HEAD_LANES, (slab // 2 + 1) * HEAD_LANES)

        def scores(slab):
            kaug = jnp.concatenate([k_ref[:, head_cols(slab)], pos_scr[...]], axis=1)
            s_scr[slab % 2] = _dot(kaug, qaug_scr[slab])

        def softmax_update(slab):
            s = s_scr[slab % 2]
            if on_diagonal:
                s = jnp.where(visible, s, NEG_BIG)
            m_prev = m_scr[slab]
            m_new = jnp.maximum(m_prev, jnp.max(s, axis=0, keepdims=True))
            alpha = jnp.exp(m_prev - m_new)
            p = jnp.exp(s - m_new)
            l_scr[slab] = alpha * l_scr[slab] + jnp.sum(p, axis=0, keepdims=True)
            acc_scr[slab] = alpha * acc_scr[slab] + _dot(vt_ref[head_cols(slab), :],
                                                         p.astype(BF16))
            m_scr[slab] = m_new

        n_slabs = 2 * n_heads
        scores(0)
        for slab in range(n_slabs):
            if slab + 1 < n_slabs:
                scores(slab + 1)
            softmax_update(slab)

    @pl.when(ki < qi)
    def _below_diagonal():
        step(False)

    @pl.when(ki == qi)
    def _diagonal_and_finalize():
        step(True)
        lam = _lambda_value(lq1_ref, lk1_ref, lq2_ref, lk2_ref, lambda_init)
        for h in range(n_heads):
            o = (acc_scr[2 * h] * (1.0 / l_scr[2 * h])
                 - lam * (acc_scr[2 * h + 1] * (1.0 / l_scr[2 * h + 1])))
            ms = jnp.mean(o * o, axis=0, keepdims=True)
            o = o * lax.rsqrt(ms + LN_EPS) * g_ref[...] * (1.0 - lambda_init)
            o_ref[:, h * HEAD_LANES:(h + 1) * HEAD_LANES] = o.T.astype(o_ref.dtype)


def _prompt_attention(qt_b, k_b, vt_b, lam_params, subln_g, lambda_init, tq=512):
    bsz, width, seq = qt_b.shape
    n_heads = width // HEAD_LANES
    tk = tq
    nq = seq // tq
    assert (seq - 1) // ALIBI_POS_SPLIT < 256, "key position parts must be exact in bf16"
    lq1, lk1, lq2, lk2 = [p.reshape(1, HEAD_DIM) for p in lam_params]
    pairs = [(i, j) for i in range(nq) for j in range(i + 1)]
    qi_tab = jnp.asarray([p[0] for p in pairs], jnp.int32)
    ki_tab = jnp.asarray([p[1] for p in pairs], jnp.int32)
    qt_spec = pl.BlockSpec((None, width, tq), lambda b, s, qi, ki: (b, 0, qi[s]))
    k_spec = pl.BlockSpec((tk, width), lambda b, s, qi, ki: (b * nq + ki[s], 0))
    vt_spec = pl.BlockSpec((None, width, tk), lambda b, s, qi, ki: (b, 0, ki[s]))
    o_spec = pl.BlockSpec((tq, width), lambda b, s, qi, ki: (b * nq + qi[s], 0))
    small = lambda r, w: pl.BlockSpec((r, w), lambda b, s, qi, ki: (0, 0))
    grid_spec = pltpu.PrefetchScalarGridSpec(
        num_scalar_prefetch=2,
        grid=(bsz, len(pairs)),
        in_specs=[qt_spec, k_spec, vt_spec,
                  small(1, HEAD_DIM), small(1, HEAD_DIM), small(1, HEAD_DIM), small(1, HEAD_DIM),
                  small(HEAD_LANES, 1)],
        out_specs=o_spec,
        scratch_shapes=[pltpu.VMEM((2 * n_heads, 2 * HEAD_LANES, tq), BF16),
                        pltpu.VMEM((tk, LANE), BF16),
                        pltpu.VMEM((2, tk, tq), F32),
                        pltpu.VMEM((2 * n_heads, 1, tq), F32),
                        pltpu.VMEM((2 * n_heads, 1, tq), F32),
                        pltpu.VMEM((2 * n_heads, HEAD_LANES, tq), F32)],
    )
    return pl.pallas_call(
        functools.partial(_prompt_attn_kernel, n_heads=n_heads, tq=tq, tk=tk,
                          lambda_init=lambda_init),
        grid_spec=grid_spec,
        out_shape=jax.ShapeDtypeStruct((bsz * seq, width), BF16),
        compiler_params=pltpu.CompilerParams(
            dimension_semantics=("arbitrary", "arbitrary"), vmem_limit_bytes=VMEM_LIMIT),
        name="prompt_attention",
    )(qi_tab, ki_tab, qt_b, k_b, vt_b, lq1, lk1, lq2, lk2, subln_g.reshape(HEAD_LANES, 1))


def _sample_attn_kernel(pt_ref, q_ref, kn_ref, vn_ref, lq1_ref, lk1_ref, lq2_ref, lk2_ref,
                        g_ref, *rest, n_pages, page, n_new, n_heads, lambda_init):
    kt_refs = rest[:n_pages]
    v_refs = rest[n_pages:2 * n_pages]
    o_ref = rest[2 * n_pages]
    s_scr = rest[2 * n_pages + 1]
    del pt_ref
    width = n_heads * HEAD_LANES
    n_maps = 2 * n_heads
    rows = n_new * n_maps
    past = n_pages * page

    q = q_ref[0].astype(F32)
    mp = lax.broadcasted_iota(jnp.int32, (n_maps, width), 0)
    cl = lax.broadcasted_iota(jnp.int32, (n_maps, width), 1)
    map_lanes = (cl // HEAD_DIM) == mp
    qbd32 = jnp.concatenate(
        [jnp.where(map_lanes, jnp.broadcast_to(q[i:i + 1, :], (n_maps, width)), 0.0)
         for i in range(n_new)], axis=0)
    qbd = qbd32.astype(BF16)

    for p in range(n_pages):
        s_scr[:, p * page:(p + 1) * page] = _dot(qbd, kt_refs[p][...].astype(BF16))

    rmap = lax.broadcasted_iota(jnp.int32, (rows, 1), 0) % n_maps
    rqry = lax.broadcasted_iota(jnp.int32, (rows, 1), 0) // n_maps
    slope = jnp.zeros((rows, 1), F32)
    for h in range(n_heads):
        slope = jnp.where(rmap // 2 == h, _alibi_slope(h, n_heads), slope)
    tpos = lax.broadcasted_iota(jnp.int32, (1, past), 1).astype(F32)
    s = s_scr[...] + slope * tpos

    kn = kn_ref[0]
    vn = vn_ref[0]
    s_new = []
    for j in range(n_new):
        sj = jnp.sum(qbd32 * kn[j:j + 1, :], axis=-1, keepdims=True) + slope * float(past + j)
        s_new.append(jnp.where(rqry >= j, sj, NEG_BIG))
    m = jnp.max(s, axis=-1, keepdims=True)
    for sj in s_new:
        m = jnp.maximum(m, sj)
    pr = jnp.exp(s - m)
    l = jnp.sum(pr, axis=-1, keepdims=True)
    prb = pr.astype(BF16)
    p_new = []
    for sj in s_new:
        pj = jnp.exp(sj - m)
        l = l + pj
        p_new.append(pj)
    inv_l = 1.0 / l

    lam = _lambda_value(lq1_ref, lk1_ref, lq2_ref, lk2_ref, lambda_init)
    mrow = lax.broadcasted_iota(jnp.int32, (n_maps, 1), 0)
    for h in range(n_heads):
        cols = slice(h * HEAD_LANES, (h + 1) * HEAD_LANES)
        out = jnp.zeros((rows, HEAD_LANES), F32)
        for j, pj in enumerate(p_new):
            out = out + pj * vn[j:j + 1, cols]
        for p in range(n_pages):
            vh = v_refs[p][pl.ds(h, page, stride=n_heads), :].astype(BF16)
            out = out + _dot(prb[:, p * page:(p + 1) * page], vh)
        out = out * inv_l
        weight = jnp.where(mrow == 2 * h, 1.0, jnp.where(mrow == 2 * h + 1, -lam, 0.0))
        for i in range(n_new):
            o = jnp.sum(out[i * n_maps:(i + 1) * n_maps, :] * weight, axis=0, keepdims=True)
            o_ref[0, i:i + 1, cols] = _sub_ln(o, g_ref[...], lambda_init)
    o_ref[0, n_new:, :] = jnp.zeros((SUBLANE - n_new, width), F32)


def _sample_attention(q8, kn8, vn8, cache_kt, cache_v, page_table, lam_params, subln_g,
                      n_new, lambda_init):
    n_seq, _, width = q8.shape
    n_pages = page_table.shape[1]
    page = cache_kt.shape[2]
    n_heads = width // HEAD_LANES
    assert 2 * n_heads == SUBLANE and n_new <= SUBLANE
    lq1, lk1, lq2, lk2 = [p.reshape(1, HEAD_DIM) for p in lam_params]
    seq_spec = pl.BlockSpec((1, SUBLANE, width), lambda s, pt: (s, 0, 0))
    small = lambda w: pl.BlockSpec((1, w), lambda s, pt: (0, 0))

    def page_spec(p, shape):
        return pl.BlockSpec((None,) + shape, lambda s, pt: (pt[s * n_pages + p], 0, 0))

    grid_spec = pltpu.PrefetchScalarGridSpec(
        num_scalar_prefetch=1,
        grid=(n_seq,),
        in_specs=[seq_spec, seq_spec, seq_spec,
                  small(HEAD_DIM), small(HEAD_DIM), small(HEAD_DIM), small(HEAD_DIM),
                  small(HEAD_LANES)]
                 + [page_spec(p, cache_kt.shape[1:]) for p in range(n_pages)]
                 + [page_spec(p, cache_v.shape[1:]) for p in range(n_pages)],
        out_specs=seq_spec,
        scratch_shapes=[pltpu.VMEM((n_new * 2 * n_heads, n_pages * page), F32)],
    )
    return pl.pallas_call(
        functools.partial(_sample_attn_kernel, n_pages=n_pages, page=page, n_new=n_new,
                          n_heads=n_heads, lambda_init=lambda_init),
        grid_spec=grid_spec,
        out_shape=jax.ShapeDtypeStruct((n_seq, SUBLANE, width), F32),
        compiler_params=pltpu.CompilerParams(
            dimension_semantics=("arbitrary",), vmem_limit_bytes=VMEM_LIMIT),
        name="sample_attention",
    )(page_table.reshape(-1), q8, kn8, vn8, lq1, lk1, lq2, lk2,
      subln_g.reshape(1, HEAD_LANES), *([cache_kt] * n_pages), *([cache_v] * n_pages))


def _prompt_conv_kernel(u_ref, w_ref, b_ref, lg_ref, lb_ref, o_ref, buf, wrep, cv_scr, *,
                        tt, taps, halo, rc):
    t = pl.program_id(1)
    n_slab = buf.shape[0]

    @pl.when(t == 0)
    def _start_of_sequence():
        buf[:, 0:halo, :] = jnp.zeros((n_slab, halo, LANE), F32)
        for s in range(n_slab):
            cols = slice(s * LANE, (s + 1) * LANE)
            wrep[taps, s] = jnp.broadcast_to(b_ref[:, cols], (SUBLANE, LANE))
            for j in range(taps):
                wrep[j, s] = jnp.broadcast_to(w_ref[j:j + 1, cols], (SUBLANE, LANE))

    for s in range(n_slab):
        buf[s, halo:halo + tt, :] = u_ref[:, s * LANE:(s + 1) * LANE]
    base = halo - (taps - 1)
    groups = rc // SUBLANE

    def block_conv(i, carry):
        s = i % n_slab
        r0 = pl.multiple_of((i // n_slab) * rc, rc)
        acc = jnp.broadcast_to(wrep[taps, s][None], (groups, SUBLANE, LANE))
        for phase in range(SUBLANE):
            phase_taps = range(phase, taps, SUBLANE)
            n_groups = groups + len(phase_taps) - 1
            x = buf[s, pl.ds(base + r0 + phase, n_groups * SUBLANE), :].reshape(
                n_groups, SUBLANE, LANE)
            for a, j in enumerate(phase_taps):
                acc = acc + wrep[j, s][None] * x[a:a + groups]
        cv_scr[s, pl.ds(r0, rc), :] = acc.reshape(rc, LANE)
        return carry

    lax.fori_loop(0, (tt // rc) * n_slab, block_conv, 0)
    cv = jnp.concatenate([cv_scr[s] for s in range(n_slab)], axis=1)
    o_ref[...] = _silu(_ln(cv, lg_ref[...], lb_ref[...])).astype(o_ref.dtype)
    for s in range(n_slab):
        buf[s, 0:halo, :] = buf[s, tt:tt + halo, :]


def _prompt_conv(u, conv_w, conv_b, ln_g, ln_b, bsz, seq, tt=512, rc=128):
    n, width = u.shape
    taps = conv_w.shape[0]
    halo = -(-(taps - 1) // SUBLANE) * SUBLANE
    nt = seq // tt
    row_spec = pl.BlockSpec((tt, width), lambda b, t: (b * nt + t, 0))
    const = lambda r: pl.BlockSpec((r, width), lambda b, t: (0, 0))
    return pl.pallas_call(
        functools.partial(_prompt_conv_kernel, tt=tt, taps=taps, halo=halo, rc=rc),
        grid=(bsz, nt),
        in_specs=[row_spec, const(taps), const(1), const(1), const(1)],
        out_specs=row_spec,
        out_shape=jax.ShapeDtypeStruct((n, width), BF16),
        scratch_shapes=[pltpu.VMEM((width // LANE, halo + tt, LANE), F32),
                        pltpu.VMEM((taps + 1, width // LANE, SUBLANE, LANE), F32),
                        pltpu.VMEM((width // LANE, tt, LANE), F32)],
        compiler_params=pltpu.CompilerParams(
            dimension_semantics=("arbitrary", "arbitrary"), vmem_limit_bytes=VMEM_LIMIT),
        name="prompt_conv",
    )(u, conv_w, conv_b.reshape(1, width), ln_g.reshape(1, width), ln_b.reshape(1, width))


def _sample_conv_kernel(full_ref, w_ref, b_ref, lg_ref, lb_ref, o_ref, *, n_new, n_seq, taps):
    for t in range(n_new):
        cv = jnp.broadcast_to(b_ref[...], (n_seq, full_ref.shape[2]))
        for j in range(taps):
            cv = cv + w_ref[j:j + 1, :] * full_ref[t + j]
        o_ref[t] = _silu(_ln(cv, lg_ref[...], lb_ref[...])).astype(o_ref.dtype)


def _sample_conv(full_tm, conv_w, conv_b, ln_g, ln_b, n_new):
    n_time, n_seq, width = full_tm.shape
    taps = conv_w.shape[0]
    return pl.pallas_call(
        functools.partial(_sample_conv_kernel, n_new=n_new, n_seq=n_seq, taps=taps),
        out_shape=jax.ShapeDtypeStruct((n_new, n_seq, width), BF16),
        compiler_params=pltpu.CompilerParams(vmem_limit_bytes=VMEM_LIMIT),
        name="sample_conv",
    )(full_tm, conv_w, conv_b.reshape(1, width), ln_g.reshape(1, width), ln_b.reshape(1, width))


def _ffn_kernel(*refs, has_state, tm, pad, tshift, fc, d_ff, alpha):
    (xp_ref, attn_ref, conv_ref, g1_ref, sh2_ref, sc2_ref, g2_ref,
     wout_ref, l1g_ref, l1b_ref, wup_ref, cw_ref, cb_ref, wdn_ref, l2g_ref, l2b_ref) = refs[:16]
    refs = refs[16:]
    if has_state:
        state_ref, refs = refs[0], refs[1:]
    y_ref, tail_ref, ubuf, acc_ref = refs
    half = attn_ref.shape[1]
    n_slab = fc // LANE

    if has_state:
        hist_ref = state_ref
    else:
        hist_ref = tail_ref

        @pl.when(pl.program_id(1) == 0)
        def _zero_history():
            tail_ref[...] = jnp.zeros(tail_ref.shape, F32)

    def mod_rows(ref):
        m = ref[...]
        reps = 1 if m.shape[0] == 1 else tm // m.shape[0]
        return m if reps == 1 else jnp.concatenate([m] * reps, axis=0)

    mix = _dot(attn_ref[...], wout_ref[0:half, :]) + _dot(conv_ref[...], wout_ref[half:, :])
    x1 = _ln(alpha * xp_ref[...] + mod_rows(g1_ref) * mix, l1g_ref[...], l1b_ref[...])
    h2 = (x1 * (1.0 + mod_rows(sc2_ref)) + mod_rows(sh2_ref)).astype(BF16)

    n_chunks = d_ff // fc

    def up_project(c):
        for part in range(2):
            col = part * d_ff + c * fc
            up = _dot(h2, wup_ref[:, col:col + fc])
            for s in range(n_slab):
                cols = slice(col + s * LANE, col + (s + 1) * LANE)
                slab = ((c % 2) * 2 + part) * n_slab + s
                ubuf[slab, 0:pad, :] = hist_ref[:, cols]
                ubuf[slab, pad:pad + tm, :] = up[:, s * LANE:(s + 1) * LANE]
            tail_ref[:, col:col + fc] = up[tm - pad:, :]

    def conv_half(c, part):
        parts = []
        for s in range(n_slab):
            col = part * d_ff + c * fc + s * LANE
            cols = slice(col, col + LANE)
            slab = ((c % 2) * 2 + part) * n_slab + s
            parts.append(cw_ref[0:1, cols] * ubuf[slab, pad - 2 * tshift:pad - 2 * tshift + tm, :]
                         + cw_ref[1:2, cols] * ubuf[slab, pad - tshift:pad - tshift + tm, :]
                         + cw_ref[2:3, cols] * ubuf[slab, pad:pad + tm, :] + cb_ref[:, cols])
        return jnp.concatenate(parts, axis=1)

    up_project(0)
    for c in range(n_chunks):
        if c + 1 < n_chunks:
            up_project(c + 1)
        g = (_silu(conv_half(c, 0)) * conv_half(c, 1)).astype(BF16)
        contrib = _dot(g, wdn_ref[c * fc:(c + 1) * fc, :])
        if c == 0:
            acc_ref[...] = contrib
        else:
            acc_ref[...] += contrib

    y_ref[...] = _ln(alpha * x1 + mod_rows(g2_ref) * acc_ref[...], l2g_ref[...], l2b_ref[...])


def _ffn(xp2d, attn, conv, mod, w_out_bf, ln1, w_up_bf, ffn_conv_w, ffn_conv_b,
         w_down_bf, ln2, *, groups, tiles, tm, pad, tshift, state, alpha, fc=256):
    n, d = xp2d.shape
    half = attn.shape[1]
    d_ff = w_down_bf.shape[0]
    r = mod.shape[1]
    has_state = state is not None
    assert pad == 2 * tshift or not has_state
    assert tm >= pad
    row = lambda w: pl.BlockSpec((tm, w), lambda b, t: (b * tiles + t, 0))
    mod_spec = lambda comp: pl.BlockSpec((None, r, d), lambda b, t: (b, 0, comp))
    vec = lambda a: a.reshape(1, -1)
    args = [xp2d, attn, conv, mod, mod, mod, mod,
            w_out_bf, vec(ln1[0]), vec(ln1[1]), w_up_bf, ffn_conv_w, vec(ffn_conv_b),
            w_down_bf, vec(ln2[0]), vec(ln2[1])]
    in_specs = [row(d), row(half), row(half), mod_spec(2), mod_spec(3), mod_spec(4), mod_spec(5),
                _const_spec(w_out_bf.shape), _const_spec((1, d)), _const_spec((1, d)),
                _const_spec(w_up_bf.shape), _const_spec(ffn_conv_w.shape),
                _const_spec((1, 2 * d_ff)), _const_spec(w_down_bf.shape),
                _const_spec((1, d)), _const_spec((1, d))]
    if has_state:
        args.append(state)
        in_specs.append(_const_spec(state.shape))
    y, tail = pl.pallas_call(
        functools.partial(_ffn_kernel, has_state=has_state, tm=tm, pad=pad,
                          tshift=tshift, fc=fc, d_ff=d_ff, alpha=alpha),
        grid=(groups, tiles),
        in_specs=in_specs,
        out_specs=[row(d), pl.BlockSpec((None, pad, 2 * d_ff), lambda b, t: (b, 0, 0))],
        out_shape=[jax.ShapeDtypeStruct((n, d), F32),
                   jax.ShapeDtypeStruct((groups, pad, 2 * d_ff), F32)],
        scratch_shapes=[pltpu.VMEM((4 * (fc // LANE), pad + tm, LANE), F32),
                        pltpu.VMEM((tm, d), F32)],
        compiler_params=pltpu.CompilerParams(
            dimension_semantics=("arbitrary", "arbitrary"), vmem_limit_bytes=VMEM_LIMIT),
        name="ffn",
    )(*args)
    return y, tail


def kernel(x_prompt, x_sample, c_prompt, c_sample, cache_k, cache_v, page_table, state_conv, state_ffn, ln_emb_g, ln_emb_b, w_ada, b_ada, w_in, lambda_q1, lambda_k1, lambda_q2, lambda_k2, subln_g, conv_w, conv_b, conv_ln_g, conv_ln_b, w_out, ln1_g, ln1_b, w_up, ffn_conv_w, ffn_conv_b, w_down, ln2_g, ln2_b):
    bsz, seq, d = x_prompt.shape
    n_seq, n_new, _ = x_sample.shape
    depth = w_ada.shape[0]
    assert depth == 1, "the prompt/sample activations are threaded for a single layer"
    n_phys, page = cache_k.shape[1], cache_k.shape[2]
    width = cache_k.shape[3] * cache_k.shape[4]
    d_ff = w_down.shape[1]
    conv_taps = conv_w.shape[1]
    ffn_taps = ffn_conv_w.shape[1]
    assert ffn_taps == 3
    alpha = (2 * depth) ** 0.25
    tm = 512
    n_rows_s = n_seq * n_new
    n_kh = width // HEAD_DIM
    n_vh = width // HEAD_LANES

    l = 0
    lambda_init = 0.8 - 0.6 * math.exp(-0.3 * l)
    lam_params = (lambda_q1[l], lambda_k1[l], lambda_q2[l], lambda_k2[l])
    w_in_bf = w_in[l].astype(BF16)
    w_out_bf = w_out[l].astype(BF16)
    w_up_bf = w_up[l].astype(BF16)
    w_down_bf = w_down[l].astype(BF16)

    c_rows = bsz + n_seq
    c_all = jnp.concatenate([c_prompt, c_sample], axis=0)
    c_all = jnp.pad(c_all, ((0, -c_rows % (2 * SUBLANE)), (0, 0)))
    mod = _modulation(c_all, w_ada[l], b_ada[l])
    mod_p = mod[:bsz].reshape(bsz, 1, 6 * d)
    mod_s = mod[bsz:c_rows]

    pp = _in_projection(x_prompt.reshape(bsz * seq, d), mod_p, ln_emb_g, ln_emb_b, w_in_bf,
                        groups=bsz, tm=tm,
                        names=("xp", "qt_b", "k_b", "kt", "v_heads", "vt_b", "u"))
    u = pp["u"]
    attn = _prompt_attention(pp["qt_b"], pp["k_b"], pp["vt_b"], lam_params, subln_g[l],
                             lambda_init)
    conv = _prompt_conv(u, conv_w[l], conv_b[l], conv_ln_g[l], conv_ln_b[l], bsz, seq)
    y_p, tail_p = _ffn(pp["xp"], attn, conv, mod_p, w_out_bf,
                       (ln1_g[l], ln1_b[l]), w_up_bf, ffn_conv_w[l], ffn_conv_b[l], w_down_bf,
                       (ln2_g[l], ln2_b[l]), groups=bsz, tiles=seq // tm, tm=tm,
                       pad=SUBLANE, tshift=1, state=None, alpha=alpha)
    y_prompt = y_p.reshape(bsz, seq, d)
    k_prompt = pp["kt"].reshape(bsz, n_kh, HEAD_DIM, seq).transpose(0, 3, 1, 2)[None]
    v_prompt = pp["v_heads"].reshape(1, bsz, seq, n_vh, HEAD_LANES)
    conv_prompt = u.reshape(bsz, seq, -1)[None, :, seq - (conv_taps - 1):, :]
    ffn_prompt = tail_p[None, :, SUBLANE - (ffn_taps - 1):, :]

    xs_tm = x_sample.swapaxes(0, 1).reshape(n_rows_s, d)
    ps = _in_projection(xs_tm, mod_s[None], ln_emb_g, ln_emb_b, w_in_bf, groups=n_new, tm=n_seq,
                        names=("xp", "q", "k", "kt", "v", "u"))
    qs, ks, kts, vs, us = ps["q"], ps["k"], ps["kt"], ps["v"], ps["u"]
    to_sm = lambda a: a.reshape(n_new, n_seq, -1).swapaxes(0, 1)
    pad8 = lambda a: jnp.pad(to_sm(a), ((0, 0), (0, SUBLANE - n_new), (0, 0)))
    cache_kt = cache_k[l].transpose(0, 2, 3, 1).reshape(n_phys, width, page)
    cache_vr = cache_v[l].reshape(n_phys, page * n_vh, HEAD_LANES)
    attn_s8 = _sample_attention(pad8(qs), pad8(ks), pad8(vs), cache_kt, cache_vr,
                                page_table, lam_params, subln_g[l], n_new, lambda_init)
    attn_s = attn_s8[:, :n_new, :].astype(BF16).swapaxes(0, 1).reshape(n_rows_s, width)
    full_tm = jnp.concatenate([state_conv[l].swapaxes(0, 1), us.reshape(n_new, n_seq, -1)], axis=0)
    conv_s = _sample_conv(full_tm, conv_w[l], conv_b[l], conv_ln_g[l], conv_ln_b[l],
                          n_new).reshape(n_rows_s, -1)
    state_tm = state_ffn[l].swapaxes(0, 1).reshape((ffn_taps - 1) * n_seq, 2 * d_ff)
    y_s, tail_s = _ffn(ps["xp"], attn_s, conv_s, mod_s[None],
                       w_out_bf, (ln1_g[l], ln1_b[l]), w_up_bf,
                       ffn_conv_w[l], ffn_conv_b[l], w_down_bf, (ln2_g[l], ln2_b[l]),
                       groups=1, tiles=1, tm=n_rows_s, pad=(ffn_taps - 1) * n_seq,
                       tshift=n_seq, state=state_tm, alpha=alpha)
    y_sample = y_s.reshape(n_new, n_seq, d).swapaxes(0, 1)
    k_sample = kts.reshape(n_new, n_kh, HEAD_DIM, n_seq).transpose(3, 0, 1, 2)[None]
    v_sample = to_sm(vs).reshape(1, n_seq, n_new, n_vh, HEAD_LANES)
    conv_sample = full_tm[n_new:].swapaxes(0, 1)[None]
    ffn_sample = tail_s.reshape(ffn_taps - 1, n_seq, 2 * d_ff).swapaxes(0, 1)[None]

    return (y_prompt, y_sample, k_prompt, v_prompt, conv_prompt, ffn_prompt,
            k_sample, v_sample, conv_sample, ffn_sample)
```

```python
import functools
import math

import jax
import jax.numpy as jnp
from jax import lax
from jax.experimental import pallas as pl
from jax.experimental.pallas import tpu as pltpu

F32 = jnp.float32
BF16 = jnp.bfloat16

LN_EPS = 1e-5
HEAD_DIM = 64
HEAD_LANES = 2 * HEAD_DIM
LANE = 128
SUBLANE = 8
VMEM_LIMIT = 56 * 1024 * 1024
NEG_BIG = -1e30
ALIBI_POS_SPLIT = 64

def _ln(x, g, b):
    mu = jnp.mean(x, axis=-1, keepdims=True)
    xc = x - mu
    var = jnp.mean(xc * xc, axis=-1, keepdims=True)
    return xc * lax.rsqrt(var + LN_EPS) * g + b


def _silu(x):
    return x * jax.nn.sigmoid(x)


def _dot(a, b):
    return jnp.dot(a, b, preferred_element_type=F32)


def _const_spec(shape):
    nd = len(shape)
    return pl.BlockSpec(shape, lambda *_: (0,) * nd, pipeline_mode=pl.Buffered(1))


def _alibi_slope(h, n_heads):
    return 2.0 ** (-8.0 * (h + 1) / n_heads)


def _split_bf16(x):
    hi = x.astype(BF16)
    return hi, (x - hi.astype(F32)).astype(BF16)


def _mod_kernel(c_ref, w_ref, b_ref, o_ref):
    a_hi, a_lo = _split_bf16(_silu(c_ref[...]))
    w_hi, w_lo = _split_bf16(w_ref[...])
    o_ref[...] = _dot(a_hi, w_hi) + (_dot(a_hi, w_lo) + _dot(a_lo, w_hi)) + b_ref[...]


def _modulation(c_all, w_ada, b_ada):
    rows, d = c_all.shape
    n_out = w_ada.shape[1]
    return pl.pallas_call(
        _mod_kernel,
        grid=(n_out // d,),
        in_specs=[pl.BlockSpec((rows, d), lambda j: (0, 0)),
                  pl.BlockSpec((d, d), lambda j: (0, j)),
                  pl.BlockSpec((1, d), lambda j: (0, j))],
        out_specs=pl.BlockSpec((rows, d), lambda j: (0, j)),
        out_shape=jax.ShapeDtypeStruct((rows, n_out), F32),
        compiler_params=pltpu.CompilerParams(vmem_limit_bytes=VMEM_LIMIT),
        name="modulation",
    )(c_all, w_ada, b_ada.reshape(1, n_out))


def _inproj_kernel(x_ref, sh_ref, sc_ref, lg_ref, lb_ref, w_ref, *out_refs, width, names):
    out = dict(zip(names, out_refs))
    tm = x_ref.shape[0]
    xp = _ln(x_ref[...], lg_ref[...], lb_ref[...])
    h = (xp * (1.0 + sc_ref[...]) + sh_ref[...]).astype(BF16)

    def proj(i):
        return _dot(h, w_ref[:, i * width:(i + 1) * width])

    def emit(name, value):
        if name in out:
            out[name][...] = value.astype(out[name].dtype)

    emit("xp", xp)
    q = proj(0) * (HEAD_DIM ** -0.5)
    emit("q", q)
    if "qt_b" in out:
        emit("qt_b", q.T)
    k = proj(1)
    emit("k", k)
    emit("k_b", k)
    emit("kt", k.T)
    v = proj(2)
    emit("v", v)
    if "vt_b" in out:
        emit("vt_b", v.T)
    if "v_heads" in out:
        n_heads = width // HEAD_LANES
        for hd in range(n_heads):
            out["v_heads"][pl.ds(hd, tm, stride=n_heads), :] = (
                v[:, hd * HEAD_LANES:(hd + 1) * HEAD_LANES])
    emit("u", proj(3) * jax.nn.sigmoid(proj(4)))


def _in_projection(x2d, mod, ln_g, ln_b, w_in_bf, *, groups, tm, names):
    n, d = x2d.shape
    width = w_in_bf.shape[1] // 5
    n_heads = width // HEAD_LANES
    per_group = n // groups // tm
    mod_groups, r = mod.shape[0], mod.shape[1]

    def mod_spec(comp):
        return pl.BlockSpec((None, r, d),
                            lambda i: ((i // per_group) % mod_groups, 0, comp))

    row_spec = lambda rows, w: pl.BlockSpec((rows, w), lambda i: (i, 0))
    t_spec = pl.BlockSpec((None, width, tm), lambda i: (i // per_group, 0, i % per_group))
    t_shape = (groups, width, n // groups)
    dtypes = {"xp": F32, "q": BF16, "k": F32, "v": F32, "u": F32, "k_b": BF16, "v_heads": F32,
              "kt": F32, "qt_b": BF16, "vt_b": BF16}
    out_specs, out_shape = [], []
    for nm in names:
        if nm in ("kt", "qt_b", "vt_b"):
            spec, shape = t_spec, t_shape
        elif nm == "v_heads":
            spec, shape = row_spec(tm * n_heads, HEAD_LANES), (n * n_heads, HEAD_LANES)
        elif nm == "xp":
            spec, shape = row_spec(tm, d), (n, d)
        else:
            spec, shape = row_spec(tm, width), (n, width)
        out_specs.append(spec)
        out_shape.append(jax.ShapeDtypeStruct(shape, dtypes[nm]))
    outs = pl.pallas_call(
        functools.partial(_inproj_kernel, width=width, names=tuple(names)),
        grid=(n // tm,),
        in_specs=[row_spec(tm, d), mod_spec(0), mod_spec(1),
                  _const_spec((1, d)), _const_spec((1, d)),
                  _const_spec(w_in_bf.shape)],
        out_specs=out_specs,
        out_shape=out_shape,
        compiler_params=pltpu.CompilerParams(
            dimension_semantics=("arbitrary",), vmem_limit_bytes=VMEM_LIMIT),
        name="in_projection",
    )(x2d, mod, mod, ln_g.reshape(1, d), ln_b.reshape(1, d), w_in_bf)
    return dict(zip(names, outs))


def _lambda_value(lq1_ref, lk1_ref, lq2_ref, lk2_ref, lambda_init):
    s1 = jnp.sum(lq1_ref[...] * lk1_ref[...], axis=-1, keepdims=True)
    s2 = jnp.sum(lq2_ref[...] * lk2_ref[...], axis=-1, keepdims=True)
    return jnp.exp(s1) - jnp.exp(s2) + lambda_init


def _sub_ln(o, g, lambda_init):
    ms = jnp.mean(o * o, axis=-1, keepdims=True)
    return o * lax.rsqrt(ms + LN_EPS) * g * (1.0 - lambda_init)


def _prompt_attn_kernel(qi_tab, ki_tab, qt_ref, k_ref, vt_ref, lq1_ref, lk1_ref, lq2_ref, lk2_ref,
                        g_ref, o_ref, qaug_scr, pos_scr, s_scr, m_scr, acc_scr, *,
                        n_heads, tq, tk, lambda_init):
    pair = pl.program_id(1)
    qi = qi_tab[pair]
    ki = ki_tab[pair]

    @pl.when(ki == 0)
    def _init():
        m_scr[...] = jnp.full(m_scr.shape, NEG_BIG, F32)
        acc_scr[...] = jnp.zeros(acc_scr.shape, F32)
        r = lax.broadcasted_iota(jnp.int32, (HEAD_LANES, tq), 0)
        for h in range(n_heads):
            qt = qt_ref[h * HEAD_LANES:(h + 1) * HEAD_LANES, :].astype(F32)
            slope = _alibi_slope(h, n_heads)
            bot = jnp.where(r == 0, ALIBI_POS_SPLIT * slope, jnp.where(r == 1, slope, 0.0))
            for j in range(2):
                own = (r < HEAD_DIM) if j == 0 else (r >= HEAD_DIM)
                top = jnp.where(own, qt, 0.0)
                qaug_scr[2 * h + j] = jnp.concatenate([top, bot], axis=0).astype(BF16)

    def step(on_diagonal):
        kr = lax.broadcasted_iota(jnp.int32, (tk, LANE), 0)
        lane = lax.broadcasted_iota(jnp.int32, (tk, LANE), 1)
        kpos = ki * tk + kr
        pos_hi = kpos // ALIBI_POS_SPLIT
        pos_lo = kpos % ALIBI_POS_SPLIT
        pos_scr[...] = jnp.where(lane == 0, pos_hi, jnp.where(lane == 1, pos_lo, 0)
                                 ).astype(F32).astype(BF16)
        ones_row = jnp.where(lax.broadcasted_iota(jnp.int32, (2 * SUBLANE, tk), 0) == 0,
                             1.0, 0.0).astype(BF16)
        if on_diagonal:
            key = lax.broadcasted_iota(jnp.int32, (tk, tq), 0)
            qry = lax.broadcasted_iota(jnp.int32, (tk, tq), 1)
            visible = key <= qry

        def head_cols(slab):
            return slice((slab // 2) * HEAD_LANES, (slab // 2 + 1) * HEAD_LANES)

        def scores(slab):
            kaug = jnp.concatenate([k_ref[:, head_cols(slab)], pos_scr[...]], axis=1)
            s_scr[slab % 2] = _dot(kaug, qaug_scr[slab])

        def softmax_update(slab):
            s = s_scr[slab % 2]
            if on_diagonal:
                s = jnp.where(visible, s, NEG_BIG)
            m_prev = m_scr[slab]
            m_new = jnp.maximum(m_prev, jnp.max(s, axis=0, keepdims=True))
            alpha = jnp.exp(m_prev - m_new)
            p = jnp.exp(s - m_new).astype(BF16)
            vaug = jnp.concatenate([vt_ref[head_cols(slab), :], ones_row], axis=0)
            acc_scr[slab] = alpha * acc_scr[slab] + _dot(vaug, p)
            m_scr[slab] = m_new

        n_slabs = 2 * n_heads
        scores(0)
        for slab in range(n_slabs):
            if slab + 1 < n_slabs:
                scores(slab + 1)
            softmax_update(slab)

    @pl.when(ki < qi)
    def _below_diagonal():
        step(False)

    @pl.when(ki == qi)
    def _diagonal_and_finalize():
        step(True)
        lam = _lambda_value(lq1_ref, lk1_ref, lq2_ref, lk2_ref, lambda_init)
        def normalised(slab):
            acc = acc_scr[slab]
            return acc[:HEAD_LANES] * (1.0 / acc[HEAD_LANES:HEAD_LANES + 1])

        for h in range(n_heads):
            o = normalised(2 * h) - lam * normalised(2 * h + 1)
            ms = jnp.mean(o * o, axis=0, keepdims=True)
            o = o * lax.rsqrt(ms + LN_EPS) * g_ref[...] * (1.0 - lambda_init)
            o_ref[:, h * HEAD_LANES:(h + 1) * HEAD_LANES] = o.T.astype(o_ref.dtype)


def _prompt_attention(qt_b, k_b, vt_b, lam_params, subln_g, lambda_init, tq=512):
    bsz, width, seq = qt_b.shape
    n_heads = width // HEAD_LANES
    tk = tq
    nq = seq // tq
    assert (seq - 1) // ALIBI_POS_SPLIT < 256, "key position parts must be exact in bf16"
    lq1, lk1, lq2, lk2 = [p.reshape(1, HEAD_DIM) for p in lam_params]
    pairs = [(i, j) for i in range(nq) for j in range(i + 1)]
    qi_tab = jnp.asarray([p[0] for p in pairs], jnp.int32)
    ki_tab = jnp.asarray([p[1] for p in pairs], jnp.int32)
    qt_spec = pl.BlockSpec((None, width, tq), lambda b, s, qi, ki: (b, 0, qi[s]))
    k_spec = pl.BlockSpec((tk, width), lambda b, s, qi, ki: (b * nq + ki[s], 0))
    vt_spec = pl.BlockSpec((None, width, tk), lambda b, s, qi, ki: (b, 0, ki[s]))
    o_spec = pl.BlockSpec((tq, width), lambda b, s, qi, ki: (b * nq + qi[s], 0))
    small = lambda r, w: pl.BlockSpec((r, w), lambda b, s, qi, ki: (0, 0))
    grid_spec = pltpu.PrefetchScalarGridSpec(
        num_scalar_prefetch=2,
        grid=(bsz, len(pairs)),
        in_specs=[qt_spec, k_spec, vt_spec,
                  small(1, HEAD_DIM), small(1, HEAD_DIM), small(1, HEAD_DIM), small(1, HEAD_DIM),
                  small(HEAD_LANES, 1)],
        out_specs=o_spec,
        scratch_shapes=[pltpu.VMEM((2 * n_heads, 2 * HEAD_LANES, tq), BF16),
                        pltpu.VMEM((tk, LANE), BF16),
                        pltpu.VMEM((2, tk, tq), F32),
                        pltpu.VMEM((2 * n_heads, 1, tq), F32),
                        pltpu.VMEM((2 * n_heads, HEAD_LANES + 2 * SUBLANE, tq), F32)],
    )
    return pl.pallas_call(
        functools.partial(_prompt_attn_kernel, n_heads=n_heads, tq=tq, tk=tk,
                          lambda_init=lambda_init),
        grid_spec=grid_spec,
        out_shape=jax.ShapeDtypeStruct((bsz * seq, width), BF16),
        compiler_params=pltpu.CompilerParams(
            dimension_semantics=("arbitrary", "arbitrary"), vmem_limit_bytes=VMEM_LIMIT),
        name="prompt_attention",
    )(qi_tab, ki_tab, qt_b, k_b, vt_b, lq1, lk1, lq2, lk2, subln_g.reshape(HEAD_LANES, 1))


def _sample_attn_tile(q_ref, kn_ref, vn_ref, lq1_ref, lk1_ref, lq2_ref, lk2_ref, g_ref,
                      kt_refs, v_refs, o_ref, s_scr, *, page, n_new, n_heads, lambda_init):
    n_pages = len(kt_refs)
    width = n_heads * HEAD_LANES
    n_maps = 2 * n_heads
    rows = n_new * n_maps
    past = n_pages * page

    q = q_ref[0].astype(F32)
    mp = lax.broadcasted_iota(jnp.int32, (n_maps, width), 0)
    cl = lax.broadcasted_iota(jnp.int32, (n_maps, width), 1)
    map_lanes = (cl // HEAD_DIM) == mp
    qbd32 = jnp.concatenate(
        [jnp.where(map_lanes, jnp.broadcast_to(q[i:i + 1, :], (n_maps, width)), 0.0)
         for i in range(n_new)], axis=0)
    qbd = qbd32.astype(BF16)

    for p in range(n_pages):
        s_scr[:, p * page:(p + 1) * page] = _dot(qbd, kt_refs[p][...].astype(BF16))

    rmap = lax.broadcasted_iota(jnp.int32, (rows, 1), 0) % n_maps
    rqry = lax.broadcasted_iota(jnp.int32, (rows, 1), 0) // n_maps
    slope = jnp.zeros((rows, 1), F32)
    for h in range(n_heads):
        slope = jnp.where(rmap // 2 == h, _alibi_slope(h, n_heads), slope)
    tpos = lax.broadcasted_iota(jnp.int32, (1, past), 1).astype(F32)
    s = s_scr[...] + slope * tpos

    kn = kn_ref[0]
    vn = vn_ref[0]
    s_new = []
    for j in range(n_new):
        sj = jnp.sum(qbd32 * kn[j:j + 1, :], axis=-1, keepdims=True) + slope * float(past + j)
        s_new.append(jnp.where(rqry >= j, sj, NEG_BIG))
    m = jnp.max(s, axis=-1, keepdims=True)
    for sj in s_new:
        m = jnp.maximum(m, sj)
    pr = jnp.exp(s - m)
    l = jnp.sum(pr, axis=-1, keepdims=True)
    prb = pr.astype(BF16)
    p_new = []
    for sj in s_new:
        pj = jnp.exp(sj - m)
        l = l + pj
        p_new.append(pj)
    inv_l = 1.0 / l

    lam = _lambda_value(lq1_ref, lk1_ref, lq2_ref, lk2_ref, lambda_init)
    mrow = lax.broadcasted_iota(jnp.int32, (n_maps, 1), 0)
    for h in range(n_heads):
        cols = slice(h * HEAD_LANES, (h + 1) * HEAD_LANES)
        out = jnp.zeros((rows, HEAD_LANES), F32)
        for j, pj in enumerate(p_new):
            out = out + pj * vn[j:j + 1, cols]
        for p in range(n_pages):
            vh = v_refs[p][pl.ds(h, page, stride=n_heads), :].astype(BF16)
            out = out + _dot(prb[:, p * page:(p + 1) * page], vh)
        out = out * inv_l
        weight = jnp.where(mrow == 2 * h, 1.0, jnp.where(mrow == 2 * h + 1, -lam, 0.0))
        for i in range(n_new):
            o = jnp.sum(out[i * n_maps:(i + 1) * n_maps, :] * weight, axis=0, keepdims=True)
            o_ref[0, i:i + 1, cols] = _sub_ln(o, g_ref[...], lambda_init)
    o_ref[0, n_new:, :] = jnp.zeros((SUBLANE - n_new, width), F32)


def _sample_attn_prompt_conv_kernel(pt_ref, q_ref, kn_ref, vn_ref, lq1_ref, lk1_ref, lq2_ref,
                                    lk2_ref, g_ref, u_ref, cw_ref, cb_ref, clg_ref, clb_ref,
                                    *rest, n_pages, page, n_new, n_heads, lambda_init,
                                    tiles_per_seq, tt, taps, halo):
    del pt_ref
    kt_refs = rest[:n_pages]
    v_refs = rest[n_pages:2 * n_pages]
    o_ref, conv_o_ref, s_scr, buf, wrep, cv_scr = rest[2 * n_pages:]
    first = pl.program_id(0) % tiles_per_seq == 0
    _conv_tile(first, u_ref, cw_ref, cb_ref, clg_ref, clb_ref, conv_o_ref, buf, wrep, cv_scr,
               tt=tt, taps=taps, halo=halo, rc=tt)
    _sample_attn_tile(q_ref, kn_ref, vn_ref, lq1_ref, lk1_ref, lq2_ref, lk2_ref, g_ref,
                      kt_refs, v_refs, o_ref, s_scr, page=page, n_new=n_new, n_heads=n_heads,
                      lambda_init=lambda_init)


def _sample_attention_and_prompt_conv(q8, kn8, vn8, cache_kt, cache_v, page_table, lam_params,
                                      subln_g, n_new, lambda_init,
                                      u, conv_w, conv_b, conv_ln_g, conv_ln_b, seq):
    n_seq, _, width = q8.shape
    n_pages = page_table.shape[1]
    page = cache_kt.shape[2]
    n_heads = width // HEAD_LANES
    assert 2 * n_heads == SUBLANE and n_new <= SUBLANE
    n_rows, conv_width = u.shape
    taps = conv_w.shape[0]
    halo = -(-(taps - 1) // SUBLANE) * SUBLANE
    tt = n_rows // n_seq
    assert tt * n_seq == n_rows and seq % tt == 0 and tt >= halo and tt % SUBLANE == 0
    n_slab = conv_width // LANE
    lq1, lk1, lq2, lk2 = [p.reshape(1, HEAD_DIM) for p in lam_params]
    seq_spec = pl.BlockSpec((1, SUBLANE, width), lambda s, pt: (s, 0, 0))
    small = lambda r, w: pl.BlockSpec((r, w), lambda s, pt: (0, 0))
    conv_rows = pl.BlockSpec((tt, conv_width), lambda s, pt: (s, 0))

    def page_spec(p, shape):
        return pl.BlockSpec((None,) + shape, lambda s, pt: (pt[s * n_pages + p], 0, 0))

    grid_spec = pltpu.PrefetchScalarGridSpec(
        num_scalar_prefetch=1,
        grid=(n_seq,),
        in_specs=[seq_spec, seq_spec, seq_spec,
                  small(1, HEAD_DIM), small(1, HEAD_DIM), small(1, HEAD_DIM), small(1, HEAD_DIM),
                  small(1, HEAD_LANES),
                  conv_rows, small(taps, conv_width), small(1, conv_width),
                  small(1, conv_width), small(1, conv_width)]
                 + [page_spec(p, cache_kt.shape[1:]) for p in range(n_pages)]
                 + [page_spec(p, cache_v.shape[1:]) for p in range(n_pages)],
        out_specs=[seq_spec, conv_rows],
        scratch_shapes=[pltpu.VMEM((n_new * 2 * n_heads, n_pages * page), F32),
                        pltpu.VMEM((n_slab, halo + tt, LANE), F32),
                        pltpu.VMEM((taps + 1, n_slab, SUBLANE, LANE), F32),
                        pltpu.VMEM((n_slab, tt, LANE), F32)],
    )
    vec = lambda a: a.reshape(1, -1)
    return pl.pallas_call(
        functools.partial(_sample_attn_prompt_conv_kernel, n_pages=n_pages, page=page,
                          n_new=n_new, n_heads=n_heads, lambda_init=lambda_init,
                          tiles_per_seq=seq // tt, tt=tt, taps=taps, halo=halo),
        grid_spec=grid_spec,
        out_shape=[jax.ShapeDtypeStruct((n_seq, SUBLANE, width), F32),
                   jax.ShapeDtypeStruct((n_rows, conv_width), BF16)],
        compiler_params=pltpu.CompilerParams(
            dimension_semantics=("arbitrary",), vmem_limit_bytes=VMEM_LIMIT),
        name="sample_attention_prompt_conv",
    )(page_table.reshape(-1), q8, kn8, vn8, lq1, lk1, lq2, lk2, vec(subln_g),
      u, conv_w, vec(conv_b), vec(conv_ln_g), vec(conv_ln_b),
      *([cache_kt] * n_pages), *([cache_v] * n_pages))


def _conv_tile(first, u_ref, w_ref, b_ref, lg_ref, lb_ref, o_ref, buf, wrep, cv_scr, *,
               tt, taps, halo, rc):
    n_slab = buf.shape[0]

    @pl.when(first)
    def _start_of_sequence():
        buf[:, 0:halo, :] = jnp.zeros((n_slab, halo, LANE), F32)
        for s in range(n_slab):
            cols = slice(s * LANE, (s + 1) * LANE)
            wrep[taps, s] = jnp.broadcast_to(b_ref[:, cols], (SUBLANE, LANE))
            for j in range(taps):
                wrep[j, s] = jnp.broadcast_to(w_ref[j:j + 1, cols], (SUBLANE, LANE))

    for s in range(n_slab):
        buf[s, halo:halo + tt, :] = u_ref[:, s * LANE:(s + 1) * LANE]
    base = halo - (taps - 1)
    groups = rc // SUBLANE

    def block_conv(i, carry):
        s = i % n_slab
        r0 = pl.multiple_of((i // n_slab) * rc, rc)
        acc = jnp.broadcast_to(wrep[taps, s][None], (groups, SUBLANE, LANE))
        for phase in range(SUBLANE):
            phase_taps = range(phase, taps, SUBLANE)
            n_groups = groups + len(phase_taps) - 1
            x = buf[s, pl.ds(base + r0 + phase, n_groups * SUBLANE), :].reshape(
                n_groups, SUBLANE, LANE)
            for a, j in enumerate(phase_taps):
                acc = acc + wrep[j, s][None] * x[a:a + groups]
        cv_scr[s, pl.ds(r0, rc), :] = acc.reshape(rc, LANE)
        return carry

    lax.fori_loop(0, (tt // rc) * n_slab, block_conv, 0)
    cv = jnp.concatenate([cv_scr[s] for s in range(n_slab)], axis=1)
    o_ref[...] = _silu(_ln(cv, lg_ref[...], lb_ref[...])).astype(o_ref.dtype)
    for s in range(n_slab):
        buf[s, 0:halo, :] = buf[s, tt:tt + halo, :]


def _sample_conv_kernel(full_ref, w_ref, b_ref, lg_ref, lb_ref, o_ref, *, n_new, n_seq, taps):
    for t in range(n_new):
        cv = jnp.broadcast_to(b_ref[...], (n_seq, full_ref.shape[2]))
        for j in range(taps):
            cv = cv + w_ref[j:j + 1, :] * full_ref[t + j]
        o_ref[t] = _silu(_ln(cv, lg_ref[...], lb_ref[...])).astype(o_ref.dtype)


def _sample_conv(full_tm, conv_w, conv_b, ln_g, ln_b, n_new):
    n_time, n_seq, width = full_tm.shape
    taps = conv_w.shape[0]
    return pl.pallas_call(
        functools.partial(_sample_conv_kernel, n_new=n_new, n_seq=n_seq, taps=taps),
        out_shape=jax.ShapeDtypeStruct((n_new, n_seq, width), BF16),
        compiler_params=pltpu.CompilerParams(vmem_limit_bytes=VMEM_LIMIT),
        name="sample_conv",
    )(full_tm, conv_w, conv_b.reshape(1, width), ln_g.reshape(1, width), ln_b.reshape(1, width))


def _ffn_kernel(*refs, has_state, tm, pad, tshift, fc, d_ff, alpha):
    (xp_ref, attn_ref, conv_ref, g1_ref, sh2_ref, sc2_ref, g2_ref,
     wout_ref, l1g_ref, l1b_ref, wup_ref, cw_ref, cb_ref, wdn_ref, l2g_ref, l2b_ref) = refs[:16]
    refs = refs[16:]
    if has_state:
        state_ref, refs = refs[0], refs[1:]
    y_ref, tail_ref, ubuf, acc_ref = refs
    half = attn_ref.shape[1]
    n_slab = fc // LANE

    if has_state:
        hist_ref = state_ref
    else:
        hist_ref = tail_ref

        @pl.when(pl.program_id(1) == 0)
        def _zero_history():
            tail_ref[...] = jnp.zeros(tail_ref.shape, F32)

    def mod_rows(ref):
        m = ref[...]
        reps = 1 if m.shape[0] == 1 else tm // m.shape[0]
        return m if reps == 1 else jnp.concatenate([m] * reps, axis=0)

    mix = _dot(attn_ref[...], wout_ref[0:half, :]) + _dot(conv_ref[...], wout_ref[half:, :])
    x1 = _ln(alpha * xp_ref[...] + mod_rows(g1_ref) * mix, l1g_ref[...], l1b_ref[...])
    h2 = (x1 * (1.0 + mod_rows(sc2_ref)) + mod_rows(sh2_ref)).astype(BF16)

    n_chunks = d_ff // fc

    def up_project(c):
        for part in range(2):
            col = part * d_ff + c * fc
            up = _dot(h2, wup_ref[:, col:col + fc])
            for s in range(n_slab):
                cols = slice(col + s * LANE, col + (s + 1) * LANE)
                slab = ((c % 2) * 2 + part) * n_slab + s
                ubuf[slab, 0:pad, :] = hist_ref[:, cols]
                ubuf[slab, pad:pad + tm, :] = up[:, s * LANE:(s + 1) * LANE]
            tail_ref[:, col:col + fc] = up[tm - pad:, :]

    def conv_half(c, part):
        parts = []
        for s in range(n_slab):
            col = part * d_ff + c * fc + s * LANE
            cols = slice(col, col + LANE)
            slab = ((c % 2) * 2 + part) * n_slab + s
            parts.append(cw_ref[0:1, cols] * ubuf[slab, pad - 2 * tshift:pad - 2 * tshift + tm, :]
                         + cw_ref[1:2, cols] * ubuf[slab, pad - tshift:pad - tshift + tm, :]
                         + cw_ref[2:3, cols] * ubuf[slab, pad:pad + tm, :] + cb_ref[:, cols])
        return jnp.concatenate(parts, axis=1)

    up_project(0)
    for c in range(n_chunks):
        if c + 1 < n_chunks:
            up_project(c + 1)
        g = (_silu(conv_half(c, 0)) * conv_half(c, 1)).astype(BF16)
        contrib = _dot(g, wdn_ref[c * fc:(c + 1) * fc, :])
        if c == 0:
            acc_ref[...] = contrib
        else:
            acc_ref[...] += contrib

    y_ref[...] = _ln(alpha * x1 + mod_rows(g2_ref) * acc_ref[...], l2g_ref[...], l2b_ref[...])


def _ffn(xp2d, attn, conv, mod, w_out_bf, ln1, w_up_bf, ffn_conv_w, ffn_conv_b,
         w_down_bf, ln2, *, groups, tiles, tm, pad, tshift, state, alpha, fc=256):
    n, d = xp2d.shape
    half = attn.shape[1]
    d_ff = w_down_bf.shape[0]
    r = mod.shape[1]
    has_state = state is not None
    assert pad == 2 * tshift or not has_state
    assert tm >= pad
    row = lambda w: pl.BlockSpec((tm, w), lambda b, t: (b * tiles + t, 0))
    mod_spec = lambda comp: pl.BlockSpec((None, r, d), lambda b, t: (b, 0, comp))
    vec = lambda a: a.reshape(1, -1)
    args = [xp2d, attn, conv, mod, mod, mod, mod,
            w_out_bf, vec(ln1[0]), vec(ln1[1]), w_up_bf, ffn_conv_w, vec(ffn_conv_b),
            w_down_bf, vec(ln2[0]), vec(ln2[1])]
    in_specs = [row(d), row(half), row(half), mod_spec(2), mod_spec(3), mod_spec(4), mod_spec(5),
                _const_spec(w_out_bf.shape), _const_spec((1, d)), _const_spec((1, d)),
                _const_spec(w_up_bf.shape), _const_spec(ffn_conv_w.shape),
                _const_spec((1, 2 * d_ff)), _const_spec(w_down_bf.shape),
                _const_spec((1, d)), _const_spec((1, d))]
    if has_state:
        args.append(state)
        in_specs.append(_const_spec(state.shape))
    y, tail = pl.pallas_call(
        functools.partial(_ffn_kernel, has_state=has_state, tm=tm, pad=pad,
                          tshift=tshift, fc=fc, d_ff=d_ff, alpha=alpha),
        grid=(groups, tiles),
        in_specs=in_specs,
        out_specs=[row(d), pl.BlockSpec((None, pad, 2 * d_ff), lambda b, t: (b, 0, 0))],
        out_shape=[jax.ShapeDtypeStruct((n, d), F32),
                   jax.ShapeDtypeStruct((groups, pad, 2 * d_ff), F32)],
        scratch_shapes=[pltpu.VMEM((4 * (fc // LANE), pad + tm, LANE), F32),
                        pltpu.VMEM((tm, d), F32)],
        compiler_params=pltpu.CompilerParams(
            dimension_semantics=("arbitrary", "arbitrary"), vmem_limit_bytes=VMEM_LIMIT),
        name="ffn",
    )(*args)
    return y, tail


def kernel(x_prompt, x_sample, c_prompt, c_sample, cache_k, cache_v, page_table, state_conv, state_ffn, ln_emb_g, ln_emb_b, w_ada, b_ada, w_in, lambda_q1, lambda_k1, lambda_q2, lambda_k2, subln_g, conv_w, conv_b, conv_ln_g, conv_ln_b, w_out, ln1_g, ln1_b, w_up, ffn_conv_w, ffn_conv_b, w_down, ln2_g, ln2_b):
    bsz, seq, d = x_prompt.shape
    n_seq, n_new, _ = x_sample.shape
    depth = w_ada.shape[0]
    assert depth == 1, "the prompt/sample activations are threaded for a single layer"
    n_phys, page = cache_k.shape[1], cache_k.shape[2]
    width = cache_k.shape[3] * cache_k.shape[4]
    d_ff = w_down.shape[1]
    conv_taps = conv_w.shape[1]
    ffn_taps = ffn_conv_w.shape[1]
    assert ffn_taps == 3
    alpha = (2 * depth) ** 0.25
    tm = 512
    n_rows_s = n_seq * n_new
    n_kh = width // HEAD_DIM
    n_vh = width // HEAD_LANES

    l = 0
    lambda_init = 0.8 - 0.6 * math.exp(-0.3 * l)
    lam_params = (lambda_q1[l], lambda_k1[l], lambda_q2[l], lambda_k2[l])
    w_in_bf = w_in[l].astype(BF16)
    w_out_bf = w_out[l].astype(BF16)
    w_up_bf = w_up[l].astype(BF16)
    w_down_bf = w_down[l].astype(BF16)

    c_rows = bsz + n_seq
    c_all = jnp.concatenate([c_prompt, c_sample], axis=0)
    c_all = jnp.pad(c_all, ((0, -c_rows % (2 * SUBLANE)), (0, 0)))
    mod = _modulation(c_all, w_ada[l], b_ada[l])
    mod_p = mod[:bsz].reshape(bsz, 1, 6 * d)
    mod_s = mod[bsz:c_rows]

    pp = _in_projection(x_prompt.reshape(bsz * seq, d), mod_p, ln_emb_g, ln_emb_b, w_in_bf,
                        groups=bsz, tm=tm,
                        names=("xp", "qt_b", "k_b", "kt", "v_heads", "vt_b", "u"))
    u = pp["u"]

    xs_tm = x_sample.swapaxes(0, 1).reshape(n_rows_s, d)
    ps = _in_projection(xs_tm, mod_s[None], ln_emb_g, ln_emb_b, w_in_bf, groups=n_new, tm=n_seq,
                        names=("xp", "q", "k", "kt", "v", "u"))
    qs, ks, kts, vs, us = ps["q"], ps["k"], ps["kt"], ps["v"], ps["u"]
    to_sm = lambda a: a.reshape(n_new, n_seq, -1).swapaxes(0, 1)
    pad8 = lambda a: jnp.pad(to_sm(a), ((0, 0), (0, SUBLANE - n_new), (0, 0)))
    cache_kt = cache_k[l].transpose(0, 2, 3, 1).reshape(n_phys, width, page)
    cache_vr = cache_v[l].reshape(n_phys, page * n_vh, HEAD_LANES)
    attn_s8, conv = _sample_attention_and_prompt_conv(
        pad8(qs), pad8(ks), pad8(vs), cache_kt, cache_vr, page_table, lam_params, subln_g[l],
        n_new, lambda_init, u, conv_w[l], conv_b[l], conv_ln_g[l], conv_ln_b[l], seq)
    attn_s = attn_s8[:, :n_new, :].astype(BF16).swapaxes(0, 1).reshape(n_rows_s, width)

    attn = _prompt_attention(pp["qt_b"], pp["k_b"], pp["vt_b"], lam_params, subln_g[l],
                             lambda_init)
    y_p, tail_p = _ffn(pp["xp"], attn, conv, mod_p, w_out_bf,
                       (ln1_g[l], ln1_b[l]), w_up_bf, ffn_conv_w[l], ffn_conv_b[l], w_down_bf,
                       (ln2_g[l], ln2_b[l]), groups=bsz, tiles=seq // tm, tm=tm,
                       pad=SUBLANE, tshift=1, state=None, alpha=alpha)
    y_prompt = y_p.reshape(bsz, seq, d)
    k_prompt = pp["kt"].reshape(bsz, n_kh, HEAD_DIM, seq).transpose(0, 3, 1, 2)[None]
    v_prompt = pp["v_heads"].reshape(1, bsz, seq, n_vh, HEAD_LANES)
    conv_prompt = u.reshape(bsz, seq, -1)[None, :, seq - (conv_taps - 1):, :]
    ffn_prompt = tail_p[None, :, SUBLANE - (ffn_taps - 1):, :]

    full_tm = jnp.concatenate([state_conv[l].swapaxes(0, 1), us.reshape(n_new, n_seq, -1)], axis=0)
    conv_s = _sample_conv(full_tm, conv_w[l], conv_b[l], conv_ln_g[l], conv_ln_b[l],
                          n_new).reshape(n_rows_s, -1)
    state_tm = state_ffn[l].swapaxes(0, 1).reshape((ffn_taps - 1) * n_seq, 2 * d_ff)
    y_s, tail_s = _ffn(ps["xp"], attn_s, conv_s, mod_s[None],
                       w_out_bf, (ln1_g[l], ln1_b[l]), w_up_bf,
                       ffn_conv_w[l], ffn_conv_b[l], w_down_bf, (ln2_g[l], ln2_b[l]),
                       groups=1, tiles=1, tm=n_rows_s, pad=(ffn_taps - 1) * n_seq,
                       tshift=n_seq, state=state_tm, alpha=alpha)
    y_sample = y_s.reshape(n_new, n_seq, d).swapaxes(0, 1)
    k_sample = kts.reshape(n_new, n_kh, HEAD_DIM, n_seq).transpose(3, 0, 1, 2)[None]
    v_sample = to_sm(vs).reshape(1, n_seq, n_new, n_vh, HEAD_LANES)
    conv_sample = full_tm[n_new:].swapaxes(0, 1)[None]
    ffn_sample = tail_s.reshape(ffn_taps - 1, n_seq, 2 * d_ff).swapaxes(0, 1)[None]

    return (y_prompt, y_sample, k_prompt, v_prompt, conv_prompt, ffn_prompt,
            k_sample, v_sample, conv_sample, ffn_sample)
```

```python
import functools
import math

import jax
import jax.numpy as jnp
from jax import lax
from jax.experimental import pallas as pl
from jax.experimental.pallas import tpu as pltpu

F32 = jnp.float32
BF16 = jnp.bfloat16

LN_EPS = 1e-5
HEAD_DIM = 64
HEAD_LANES = 2 * HEAD_DIM
LANE = 128
SUBLANE = 8
VMEM_LIMIT = 56 * 1024 * 1024
NEG_BIG = -1e30
ALIBI_POS_SPLIT = 64
PAGE_SLOTS = 2

def _ln(x, g, b):
    mu = jnp.mean(x, axis=-1, keepdims=True)
    xc = x - mu
    var = jnp.mean(xc * xc, axis=-1, keepdims=True)
    return xc * lax.rsqrt(var + LN_EPS) * g + b


def _silu(x):
    return x * jax.nn.sigmoid(x)


def _dot(a, b):
    return jnp.dot(a, b, preferred_element_type=F32)


def _const_spec(shape):
    nd = len(shape)
    return pl.BlockSpec(shape, lambda *_: (0,) * nd, pipeline_mode=pl.Buffered(1))


def _alibi_slope(h, n_heads):
    return 2.0 ** (-8.0 * (h + 1) / n_heads)


def _split_bf16(x):
    hi = x.astype(BF16)
    return hi, (x - hi.astype(F32)).astype(BF16)


def _mod_kernel(c_ref, w_ref, b_ref, o_ref):
    a_hi, a_lo = _split_bf16(_silu(c_ref[...]))
    w_hi, w_lo = _split_bf16(w_ref[...])
    o_ref[...] = _dot(a_hi, w_hi) + (_dot(a_hi, w_lo) + _dot(a_lo, w_hi)) + b_ref[...]


def _modulation(c_all, w_ada, b_ada):
    rows, d = c_all.shape
    n_out = w_ada.shape[1]
    return pl.pallas_call(
        _mod_kernel,
        grid=(n_out // d,),
        in_specs=[pl.BlockSpec((rows, d), lambda j: (0, 0)),
                  pl.BlockSpec((d, d), lambda j: (0, j)),
                  pl.BlockSpec((1, d), lambda j: (0, j))],
        out_specs=pl.BlockSpec((rows, d), lambda j: (0, j)),
        out_shape=jax.ShapeDtypeStruct((rows, n_out), F32),
        compiler_params=pltpu.CompilerParams(vmem_limit_bytes=VMEM_LIMIT),
        name="modulation",
    )(c_all, w_ada, b_ada.reshape(1, n_out))


def _inproj_kernel(x_ref, sh_ref, sc_ref, lg_ref, lb_ref, w_ref, *out_refs, width, names):
    out = dict(zip(names, out_refs))
    tm = x_ref.shape[0]
    xp = _ln(x_ref[...], lg_ref[...], lb_ref[...])
    h = (xp * (1.0 + sc_ref[...]) + sh_ref[...]).astype(BF16)

    def proj(i):
        return _dot(h, w_ref[:, i * width:(i + 1) * width])

    def emit(name, value):
        if name in out:
            out[name][...] = value.astype(out[name].dtype)

    emit("xp", xp)
    q = proj(0) * (HEAD_DIM ** -0.5)
    emit("q", q)
    if "qt_b" in out:
        emit("qt_b", q.T)
    k = proj(1)
    emit("k", k)
    emit("k_b", k)
    emit("kt", k.T)
    v = proj(2)
    emit("v", v)
    if "vt_b" in out:
        emit("vt_b", v.T)
    if "v_heads" in out:
        n_heads = width // HEAD_LANES
        for hd in range(n_heads):
            out["v_heads"][pl.ds(hd, tm, stride=n_heads), :] = (
                v[:, hd * HEAD_LANES:(hd + 1) * HEAD_LANES])
    emit("u", proj(3) * jax.nn.sigmoid(proj(4)))


def _in_projection(x2d, mod, ln_g, ln_b, w_in_bf, *, groups, tm, names):
    n, d = x2d.shape
    width = w_in_bf.shape[1] // 5
    n_heads = width // HEAD_LANES
    per_group = n // groups // tm
    mod_groups, r = mod.shape[0], mod.shape[1]

    def mod_spec(comp):
        return pl.BlockSpec((None, r, d),
                            lambda i: ((i // per_group) % mod_groups, 0, comp))

    row_spec = lambda rows, w: pl.BlockSpec((rows, w), lambda i: (i, 0))
    t_spec = pl.BlockSpec((None, width, tm), lambda i: (i // per_group, 0, i % per_group))
    t_shape = (groups, width, n // groups)
    dtypes = {"xp": F32, "q": BF16, "k": F32, "v": F32, "u": F32, "k_b": BF16, "v_heads": F32,
              "kt": F32, "qt_b": BF16, "vt_b": BF16}
    out_specs, out_shape = [], []
    for nm in names:
        if nm in ("kt", "qt_b", "vt_b"):
            spec, shape = t_spec, t_shape
        elif nm == "v_heads":
            spec, shape = row_spec(tm * n_heads, HEAD_LANES), (n * n_heads, HEAD_LANES)
        elif nm == "xp":
            spec, shape = row_spec(tm, d), (n, d)
        else:
            spec, shape = row_spec(tm, width), (n, width)
        out_specs.append(spec)
        out_shape.append(jax.ShapeDtypeStruct(shape, dtypes[nm]))
    outs = pl.pallas_call(
        functools.partial(_inproj_kernel, width=width, names=tuple(names)),
        grid=(n // tm,),
        in_specs=[row_spec(tm, d), mod_spec(0), mod_spec(1),
                  _const_spec((1, d)), _const_spec((1, d)),
                  _const_spec(w_in_bf.shape)],
        out_specs=out_specs,
        out_shape=out_shape,
        compiler_params=pltpu.CompilerParams(
            dimension_semantics=("arbitrary",), vmem_limit_bytes=VMEM_LIMIT),
        name="in_projection",
    )(x2d, mod, mod, ln_g.reshape(1, d), ln_b.reshape(1, d), w_in_bf)
    return dict(zip(names, outs))


def _lambda_value(lq1_ref, lk1_ref, lq2_ref, lk2_ref, lambda_init):
    s1 = jnp.sum(lq1_ref[...] * lk1_ref[...], axis=-1, keepdims=True)
    s2 = jnp.sum(lq2_ref[...] * lk2_ref[...], axis=-1, keepdims=True)
    return jnp.exp(s1) - jnp.exp(s2) + lambda_init


def _sub_ln(o, g, lambda_init):
    ms = jnp.mean(o * o, axis=-1, keepdims=True)
    return o * lax.rsqrt(ms + LN_EPS) * g * (1.0 - lambda_init)


def _prompt_attn_kernel(qi_tab, ki_tab, qt_ref, k_ref, vt_ref, lq1_ref, lk1_ref, lq2_ref, lk2_ref,
                        g_ref, o_ref, qaug_scr, pos_scr, s_scr, m_scr, acc_scr, *,
                        n_heads, tq, tk, lambda_init):
    pair = pl.program_id(1)
    qi = qi_tab[pair]
    ki = ki_tab[pair]

    @pl.when(ki == 0)
    def _init():
        m_scr[...] = jnp.full(m_scr.shape, NEG_BIG, F32)
        acc_scr[...] = jnp.zeros(acc_scr.shape, F32)
        r = lax.broadcasted_iota(jnp.int32, (HEAD_LANES, tq), 0)
        for h in range(n_heads):
            qt = qt_ref[h * HEAD_LANES:(h + 1) * HEAD_LANES, :].astype(F32)
            slope = _alibi_slope(h, n_heads)
            bot = jnp.where(r == 0, ALIBI_POS_SPLIT * slope, jnp.where(r == 1, slope, 0.0))
            for j in range(2):
                own = (r < HEAD_DIM) if j == 0 else (r >= HEAD_DIM)
                top = jnp.where(own, qt, 0.0)
                qaug_scr[2 * h + j] = jnp.concatenate([top, bot], axis=0).astype(BF16)

    def step(on_diagonal):
        kr = lax.broadcasted_iota(jnp.int32, (tk, LANE), 0)
        lane = lax.broadcasted_iota(jnp.int32, (tk, LANE), 1)
        kpos = ki * tk + kr
        pos_hi = kpos // ALIBI_POS_SPLIT
        pos_lo = kpos % ALIBI_POS_SPLIT
        pos_scr[...] = jnp.where(lane == 0, pos_hi, jnp.where(lane == 1, pos_lo, 0)
                                 ).astype(F32).astype(BF16)
        ones_row = jnp.where(lax.broadcasted_iota(jnp.int32, (2 * SUBLANE, tk), 0) == 0,
                             1.0, 0.0).astype(BF16)
        if on_diagonal:
            key = lax.broadcasted_iota(jnp.int32, (tk, tq), 0)
            qry = lax.broadcasted_iota(jnp.int32, (tk, tq), 1)
            visible = key <= qry

        def head_cols(slab):
            return slice((slab // 2) * HEAD_LANES, (slab // 2 + 1) * HEAD_LANES)

        def scores(slab):
            kaug = jnp.concatenate([k_ref[:, head_cols(slab)], pos_scr[...]], axis=1)
            s_scr[slab % 2] = _dot(kaug, qaug_scr[slab])

        def softmax_update(slab):
            s = s_scr[slab % 2]
            if on_diagonal:
                s = jnp.where(visible, s, NEG_BIG)
            m_prev = m_scr[slab]
            m_new = jnp.maximum(m_prev, jnp.max(s, axis=0, keepdims=True))
            alpha = jnp.exp(m_prev - m_new)
            p = jnp.exp(s - m_new).astype(BF16)
            vaug = jnp.concatenate([vt_ref[head_cols(slab), :], ones_row], axis=0)
            acc_scr[slab] = alpha * acc_scr[slab] + _dot(vaug, p)
            m_scr[slab] = m_new

        n_slabs = 2 * n_heads
        scores(0)
        for slab in range(n_slabs):
            if slab + 1 < n_slabs:
                scores(slab + 1)
            softmax_update(slab)

    @pl.when(ki < qi)
    def _below_diagonal():
        step(False)

    @pl.when(ki == qi)
    def _diagonal_and_finalize():
        step(True)
        lam = _lambda_value(lq1_ref, lk1_ref, lq2_ref, lk2_ref, lambda_init)
        def normalised(slab):
            acc = acc_scr[slab]
            return acc[:HEAD_LANES] * (1.0 / acc[HEAD_LANES:HEAD_LANES + 1])

        for h in range(n_heads):
            o = normalised(2 * h) - lam * normalised(2 * h + 1)
            ms = jnp.mean(o * o, axis=0, keepdims=True)
            o = o * lax.rsqrt(ms + LN_EPS) * g_ref[...] * (1.0 - lambda_init)
            o_ref[:, h * HEAD_LANES:(h + 1) * HEAD_LANES] = o.T.astype(o_ref.dtype)


def _prompt_attention(qt_b, k_b, vt_b, lam_params, subln_g, lambda_init, tq=512):
    bsz, width, seq = qt_b.shape
    n_heads = width // HEAD_LANES
    tk = tq
    nq = seq // tq
    assert (seq - 1) // ALIBI_POS_SPLIT < 256, "key position parts must be exact in bf16"
    lq1, lk1, lq2, lk2 = [p.reshape(1, HEAD_DIM) for p in lam_params]
    pairs = [(i, j) for i in range(nq) for j in range(i + 1)]
    qi_tab = jnp.asarray([p[0] for p in pairs], jnp.int32)
    ki_tab = jnp.asarray([p[1] for p in pairs], jnp.int32)
    qt_spec = pl.BlockSpec((None, width, tq), lambda b, s, qi, ki: (b, 0, qi[s]))
    k_spec = pl.BlockSpec((tk, width), lambda b, s, qi, ki: (b * nq + ki[s], 0))
    vt_spec = pl.BlockSpec((None, width, tk), lambda b, s, qi, ki: (b, 0, ki[s]))
    o_spec = pl.BlockSpec((tq, width), lambda b, s, qi, ki: (b * nq + qi[s], 0))
    small = lambda r, w: pl.BlockSpec((r, w), lambda b, s, qi, ki: (0, 0))
    grid_spec = pltpu.PrefetchScalarGridSpec(
        num_scalar_prefetch=2,
        grid=(bsz, len(pairs)),
        in_specs=[qt_spec, k_spec, vt_spec,
                  small(1, HEAD_DIM), small(1, HEAD_DIM), small(1, HEAD_DIM), small(1, HEAD_DIM),
                  small(HEAD_LANES, 1)],
        out_specs=o_spec,
        scratch_shapes=[pltpu.VMEM((2 * n_heads, 2 * HEAD_LANES, tq), BF16),
                        pltpu.VMEM((tk, LANE), BF16),
                        pltpu.VMEM((2, tk, tq), F32),
                        pltpu.VMEM((2 * n_heads, 1, tq), F32),
                        pltpu.VMEM((2 * n_heads, HEAD_LANES + 2 * SUBLANE, tq), F32)],
    )
    return pl.pallas_call(
        functools.partial(_prompt_attn_kernel, n_heads=n_heads, tq=tq, tk=tk,
                          lambda_init=lambda_init),
        grid_spec=grid_spec,
        out_shape=jax.ShapeDtypeStruct((bsz * seq, width), BF16),
        compiler_params=pltpu.CompilerParams(
            dimension_semantics=("arbitrary", "arbitrary"), vmem_limit_bytes=VMEM_LIMIT),
        name="prompt_attention",
    )(qi_tab, ki_tab, qt_b, k_b, vt_b, lq1, lk1, lq2, lk2, subln_g.reshape(HEAD_LANES, 1))


def _sample_attn_tile(q_ref, kn_ref, vn_ref, lq1_ref, lk1_ref, lq2_ref, lk2_ref, g_ref,
                      kt_refs, v_refs, o_ref, s_scr, *, page, n_new, n_heads, lambda_init):
    n_pages = len(kt_refs)
    width = n_heads * HEAD_LANES
    n_maps = 2 * n_heads
    rows = n_new * n_maps
    past = n_pages * page

    q = q_ref[0].astype(F32)
    mp = lax.broadcasted_iota(jnp.int32, (n_maps, width), 0)
    cl = lax.broadcasted_iota(jnp.int32, (n_maps, width), 1)
    map_lanes = (cl // HEAD_DIM) == mp
    qbd32 = jnp.concatenate(
        [jnp.where(map_lanes, jnp.broadcast_to(q[i:i + 1, :], (n_maps, width)), 0.0)
         for i in range(n_new)], axis=0)
    qbd = qbd32.astype(BF16)

    for p in range(n_pages):
        s_scr[:, p * page:(p + 1) * page] = _dot(qbd, kt_refs[p][...].astype(BF16))

    rmap = lax.broadcasted_iota(jnp.int32, (rows, 1), 0) % n_maps
    rqry = lax.broadcasted_iota(jnp.int32, (rows, 1), 0) // n_maps
    slope = jnp.zeros((rows, 1), F32)
    for h in range(n_heads):
        slope = jnp.where(rmap // 2 == h, _alibi_slope(h, n_heads), slope)
    tpos = lax.broadcasted_iota(jnp.int32, (1, past), 1).astype(F32)
    s = s_scr[...] + slope * tpos

    kn = kn_ref[0]
    vn = vn_ref[0]
    s_new = []
    for j in range(n_new):
        sj = jnp.sum(qbd32 * kn[j:j + 1, :], axis=-1, keepdims=True) + slope * float(past + j)
        s_new.append(jnp.where(rqry >= j, sj, NEG_BIG))
    m = jnp.max(s, axis=-1, keepdims=True)
    for sj in s_new:
        m = jnp.maximum(m, sj)
    pr = jnp.exp(s - m)
    l = jnp.sum(pr, axis=-1, keepdims=True)
    prb = pr.astype(BF16)
    p_new = []
    for sj in s_new:
        pj = jnp.exp(sj - m)
        l = l + pj
        p_new.append(pj)
    inv_l = 1.0 / l

    lam = _lambda_value(lq1_ref, lk1_ref, lq2_ref, lk2_ref, lambda_init)
    mrow = lax.broadcasted_iota(jnp.int32, (n_maps, 1), 0)
    for h in range(n_heads):
        cols = slice(h * HEAD_LANES, (h + 1) * HEAD_LANES)
        out = jnp.zeros((rows, HEAD_LANES), F32)
        for j, pj in enumerate(p_new):
            out = out + pj * vn[j:j + 1, cols]
        for p in range(n_pages):
            vh = v_refs[p][pl.ds(h, page, stride=n_heads), :].astype(BF16)
            out = out + _dot(prb[:, p * page:(p + 1) * page], vh)
        out = out * inv_l
        weight = jnp.where(mrow == 2 * h, 1.0, jnp.where(mrow == 2 * h + 1, -lam, 0.0))
        for i in range(n_new):
            o = jnp.sum(out[i * n_maps:(i + 1) * n_maps, :] * weight, axis=0, keepdims=True)
            o_ref[0, i:i + 1, cols] = _sub_ln(o, g_ref[...], lambda_init)
    o_ref[0, n_new:, :] = jnp.zeros((SUBLANE - n_new, width), F32)


def _sample_attn_prompt_conv_kernel(pt_ref, q_ref, kn_ref, vn_ref, lq1_ref, lk1_ref, lq2_ref,
                                    lk2_ref, g_ref, u_ref, cw_ref, cb_ref, clg_ref, clb_ref,
                                    kt_hbm, v_hbm, o_ref, conv_o_ref,
                                    s_scr, buf, wrep, cv_scr, kbuf, vbuf, sem, *,
                                    n_seq, n_pages, page, n_new, n_heads, lambda_init,
                                    tiles_per_seq, tt, taps, halo):
    step = pl.program_id(0)

    def page_copies(seq, slot, pages):
        copies = []
        for p in pages:
            page_id = pt_ref[seq * n_pages + p]
            copies.append(pltpu.make_async_copy(kt_hbm.at[page_id], kbuf.at[slot, p],
                                                sem.at[slot, 0, p]))
            copies.append(pltpu.make_async_copy(v_hbm.at[page_id], vbuf.at[slot, p],
                                                sem.at[slot, 1, p]))
        return copies

    all_pages = range(n_pages)

    @pl.when(step == 0)
    def _prime():
        for copy in page_copies(0, 0, all_pages):
            copy.start()

    for slot in range(PAGE_SLOTS):
        seq = step * PAGE_SLOTS + slot
        next_slot = (slot + 1) % PAGE_SLOTS
        next_seq = jnp.where(seq + 1 == n_seq, 0, seq + 1)

        def fetch_next(i, n_blocks, next_seq=next_seq, next_slot=next_slot):
            per_block = -(-n_pages // n_blocks)
            pages = range(i * per_block, min((i + 1) * per_block, n_pages))
            for copy in page_copies(next_seq, next_slot, pages):
                copy.start()

        rows = pl.ds(slot * tt, tt)
        _conv_tile(seq % tiles_per_seq == 0, u_ref.at[rows], cw_ref, cb_ref, clg_ref, clb_ref,
                   conv_o_ref.at[rows], buf, wrep, cv_scr, tt=tt, taps=taps, halo=halo, rc=tt,
                   before_block=fetch_next)
        for copy in page_copies(seq, slot, all_pages):
            copy.wait()
        one = pl.ds(slot, 1)
        _sample_attn_tile(q_ref.at[one], kn_ref.at[one], vn_ref.at[one], lq1_ref, lk1_ref,
                          lq2_ref, lk2_ref, g_ref,
                          [kbuf.at[slot, p] for p in all_pages],
                          [vbuf.at[slot, p] for p in all_pages],
                          o_ref.at[one], s_scr, page=page, n_new=n_new, n_heads=n_heads,
                          lambda_init=lambda_init)

    @pl.when(step == pl.num_programs(0) - 1)
    def _drain():
        for copy in page_copies(0, 0, all_pages):
            copy.wait()


def _sample_attention_and_prompt_conv(q8, kn8, vn8, cache_kt, cache_v, page_table, lam_params,
                                      subln_g, n_new, lambda_init,
                                      u, conv_w, conv_b, conv_ln_g, conv_ln_b, seq):
    n_seq, _, width = q8.shape
    n_pages = page_table.shape[1]
    page = cache_kt.shape[2]
    n_heads = width // HEAD_LANES
    assert 2 * n_heads == SUBLANE and n_new <= SUBLANE and n_seq % PAGE_SLOTS == 0
    n_rows, conv_width = u.shape
    taps = conv_w.shape[0]
    halo = -(-(taps - 1) // SUBLANE) * SUBLANE
    tt = n_rows // n_seq
    assert tt * n_seq == n_rows and seq % tt == 0 and tt >= halo and tt % SUBLANE == 0
    n_slab = conv_width // LANE
    lq1, lk1, lq2, lk2 = [p.reshape(1, HEAD_DIM) for p in lam_params]
    seq_spec = pl.BlockSpec((PAGE_SLOTS, SUBLANE, width), lambda s, pt: (s, 0, 0))
    small = lambda r, w: pl.BlockSpec((r, w), lambda s, pt: (0, 0))
    conv_rows = pl.BlockSpec((PAGE_SLOTS * tt, conv_width), lambda s, pt: (s, 0))
    hbm = pl.BlockSpec(memory_space=pl.ANY)

    grid_spec = pltpu.PrefetchScalarGridSpec(
        num_scalar_prefetch=1,
        grid=(n_seq // PAGE_SLOTS,),
        in_specs=[seq_spec, seq_spec, seq_spec,
                  small(1, HEAD_DIM), small(1, HEAD_DIM), small(1, HEAD_DIM), small(1, HEAD_DIM),
                  small(1, HEAD_LANES),
                  conv_rows, small(taps, conv_width), small(1, conv_width),
                  small(1, conv_width), small(1, conv_width), hbm, hbm],
        out_specs=[seq_spec, conv_rows],
        scratch_shapes=[pltpu.VMEM((n_new * 2 * n_heads, n_pages * page), F32),
                        pltpu.VMEM((n_slab, halo + tt, LANE), F32),
                        pltpu.VMEM((taps + 1, n_slab, SUBLANE, LANE), F32),
                        pltpu.VMEM((n_slab, tt, LANE), F32),
                        pltpu.VMEM((PAGE_SLOTS, n_pages) + cache_kt.shape[1:], F32),
                        pltpu.VMEM((PAGE_SLOTS, n_pages) + cache_v.shape[1:], F32),
                        pltpu.SemaphoreType.DMA((PAGE_SLOTS, 2, n_pages))],
    )
    vec = lambda a: a.reshape(1, -1)
    return pl.pallas_call(
        functools.partial(_sample_attn_prompt_conv_kernel, n_seq=n_seq, n_pages=n_pages,
                          page=page, n_new=n_new, n_heads=n_heads, lambda_init=lambda_init,
                          tiles_per_seq=seq // tt, tt=tt, taps=taps, halo=halo),
        grid_spec=grid_spec,
        out_shape=[jax.ShapeDtypeStruct((n_seq, SUBLANE, width), F32),
                   jax.ShapeDtypeStruct((n_rows, conv_width), BF16)],
        compiler_params=pltpu.CompilerParams(
            dimension_semantics=("arbitrary",), vmem_limit_bytes=VMEM_LIMIT),
        name="sample_attention_prompt_conv",
    )(page_table.reshape(-1), q8, kn8, vn8, lq1, lk1, lq2, lk2, vec(subln_g),
      u, conv_w, vec(conv_b), vec(conv_ln_g), vec(conv_ln_b), cache_kt, cache_v)


def _conv_tile(first, u_ref, w_ref, b_ref, lg_ref, lb_ref, o_ref, buf, wrep, cv_scr, *,
               tt, taps, halo, rc, before_block):
    n_slab = buf.shape[0]

    @pl.when(first)
    def _start_of_sequence():
        buf[:, 0:halo, :] = jnp.zeros((n_slab, halo, LANE), F32)
        for s in range(n_slab):
            cols = slice(s * LANE, (s + 1) * LANE)
            wrep[taps, s] = jnp.broadcast_to(b_ref[:, cols], (SUBLANE, LANE))
            for j in range(taps):
                wrep[j, s] = jnp.broadcast_to(w_ref[j:j + 1, cols], (SUBLANE, LANE))

    for s in range(n_slab):
        buf[s, halo:halo + tt, :] = u_ref[:, s * LANE:(s + 1) * LANE]
    base = halo - (taps - 1)
    groups = rc // SUBLANE

    def block_conv(i):
        s = i % n_slab
        r0 = (i // n_slab) * rc
        acc = jnp.broadcast_to(wrep[taps, s][None], (groups, SUBLANE, LANE))
        for phase in range(SUBLANE):
            phase_taps = range(phase, taps, SUBLANE)
            n_groups = groups + len(phase_taps) - 1
            x = buf[s, pl.ds(base + r0 + phase, n_groups * SUBLANE), :].reshape(
                n_groups, SUBLANE, LANE)
            for a, j in enumerate(phase_taps):
                acc = acc + wrep[j, s][None] * x[a:a + groups]
        cv_scr[s, pl.ds(r0, rc), :] = acc.reshape(rc, LANE)

    n_blocks = (tt // rc) * n_slab
    for i in range(n_blocks):
        before_block(i, n_blocks)
        block_conv(i)
    cv = jnp.concatenate([cv_scr[s] for s in range(n_slab)], axis=1)
    o_ref[...] = _silu(_ln(cv, lg_ref[...], lb_ref[...])).astype(o_ref.dtype)
    for s in range(n_slab):
        buf[s, 0:halo, :] = buf[s, tt:tt + halo, :]


def _sample_conv_kernel(full_ref, w_ref, b_ref, lg_ref, lb_ref, o_ref, *, n_new, n_seq, taps):
    for t in range(n_new):
        cv = jnp.broadcast_to(b_ref[...], (n_seq, full_ref.shape[2]))
        for j in range(taps):
            cv = cv + w_ref[j:j + 1, :] * full_ref[t + j]
        o_ref[t] = _silu(_ln(cv, lg_ref[...], lb_ref[...])).astype(o_ref.dtype)


def _sample_conv(full_tm, conv_w, conv_b, ln_g, ln_b, n_new):
    n_time, n_seq, width = full_tm.shape
    taps = conv_w.shape[0]
    return pl.pallas_call(
        functools.partial(_sample_conv_kernel, n_new=n_new, n_seq=n_seq, taps=taps),
        out_shape=jax.ShapeDtypeStruct((n_new, n_seq, width), BF16),
        compiler_params=pltpu.CompilerParams(vmem_limit_bytes=VMEM_LIMIT),
        name="sample_conv",
    )(full_tm, conv_w, conv_b.reshape(1, width), ln_g.reshape(1, width), ln_b.reshape(1, width))


def _ffn_kernel(*refs, has_state, tm, pad, tshift, fc, d_ff, alpha):
    (xp_ref, attn_ref, conv_ref, g1_ref, sh2_ref, sc2_ref, g2_ref,
     wout_ref, l1g_ref, l1b_ref, wup_ref, cw_ref, cb_ref, wdn_ref, l2g_ref, l2b_ref) = refs[:16]
    refs = refs[16:]
    if has_state:
        state_ref, refs = refs[0], refs[1:]
    y_ref, tail_ref, ubuf, acc_ref = refs
    half = attn_ref.shape[1]
    n_slab = fc // LANE

    if has_state:
        hist_ref = state_ref
    else:
        hist_ref = tail_ref

        @pl.when(pl.program_id(1) == 0)
        def _zero_history():
            tail_ref[...] = jnp.zeros(tail_ref.shape, F32)

    def mod_rows(ref):
        m = ref[...]
        reps = 1 if m.shape[0] == 1 else tm // m.shape[0]
        return m if reps == 1 else jnp.concatenate([m] * reps, axis=0)

    mix = _dot(attn_ref[...], wout_ref[0:half, :]) + _dot(conv_ref[...], wout_ref[half:, :])
    x1 = _ln(alpha * xp_ref[...] + mod_rows(g1_ref) * mix, l1g_ref[...], l1b_ref[...])
    h2 = (x1 * (1.0 + mod_rows(sc2_ref)) + mod_rows(sh2_ref)).astype(BF16)

    n_chunks = d_ff // fc

    def up_project(c):
        for part in range(2):
            col = part * d_ff + c * fc
            up = _dot(h2, wup_ref[:, col:col + fc])
            for s in range(n_slab):
                cols = slice(col + s * LANE, col + (s + 1) * LANE)
                slab = ((c % 2) * 2 + part) * n_slab + s
                ubuf[slab, 0:pad, :] = hist_ref[:, cols]
                ubuf[slab, pad:pad + tm, :] = up[:, s * LANE:(s + 1) * LANE]
            tail_ref[:, col:col + fc] = up[tm - pad:, :]

    def conv_half(c, part):
        parts = []
        for s in range(n_slab):
            col = part * d_ff + c * fc + s * LANE
            cols = slice(col, col + LANE)
            slab = ((c % 2) * 2 + part) * n_slab + s
            parts.append(cw_ref[0:1, cols] * ubuf[slab, pad - 2 * tshift:pad - 2 * tshift + tm, :]
                         + cw_ref[1:2, cols] * ubuf[slab, pad - tshift:pad - tshift + tm, :]
                         + cw_ref[2:3, cols] * ubuf[slab, pad:pad + tm, :] + cb_ref[:, cols])
        return jnp.concatenate(parts, axis=1)

    up_project(0)
    for c in range(n_chunks):
        if c + 1 < n_chunks:
            up_project(c + 1)
        g = (_silu(conv_half(c, 0)) * conv_half(c, 1)).astype(BF16)
        contrib = _dot(g, wdn_ref[c * fc:(c + 1) * fc, :])
        if c == 0:
            acc_ref[...] = contrib
        else:
            acc_ref[...] += contrib

    y_ref[...] = _ln(alpha * x1 + mod_rows(g2_ref) * acc_ref[...], l2g_ref[...], l2b_ref[...])


def _ffn(xp2d, attn, conv, mod, w_out_bf, ln1, w_up_bf, ffn_conv_w, ffn_conv_b,
         w_down_bf, ln2, *, groups, tiles, tm, pad, tshift, state, alpha, fc=256):
    n, d = xp2d.shape
    half = attn.shape[1]
    d_ff = w_down_bf.shape[0]
    r = mod.shape[1]
    has_state = state is not None
    assert pad == 2 * tshift or not has_state
    assert tm >= pad
    row = lambda w: pl.BlockSpec((tm, w), lambda b, t: (b * tiles + t, 0))
    mod_spec = lambda comp: pl.BlockSpec((None, r, d), lambda b, t: (b, 0, comp))
    vec = lambda a: a.reshape(1, -1)
    args = [xp2d, attn, conv, mod, mod, mod, mod,
            w_out_bf, vec(ln1[0]), vec(ln1[1]), w_up_bf, ffn_conv_w, vec(ffn_conv_b),
            w_down_bf, vec(ln2[0]), vec(ln2[1])]
    in_specs = [row(d), row(half), row(half), mod_spec(2), mod_spec(3), mod_spec(4), mod_spec(5),
                _const_spec(w_out_bf.shape), _const_spec((1, d)), _const_spec((1, d)),
                _const_spec(w_up_bf.shape), _const_spec(ffn_conv_w.shape),
                _const_spec((1, 2 * d_ff)), _const_spec(w_down_bf.shape),
                _const_spec((1, d)), _const_spec((1, d))]
    if has_state:
        args.append(state)
        in_specs.append(_const_spec(state.shape))
    y, tail = pl.pallas_call(
        functools.partial(_ffn_kernel, has_state=has_state, tm=tm, pad=pad,
                          tshift=tshift, fc=fc, d_ff=d_ff, alpha=alpha),
        grid=(groups, tiles),
        in_specs=in_specs,
        out_specs=[row(d), pl.BlockSpec((None, pad, 2 * d_ff), lambda b, t: (b, 0, 0))],
        out_shape=[jax.ShapeDtypeStruct((n, d), F32),
                   jax.ShapeDtypeStruct((groups, pad, 2 * d_ff), F32)],
        scratch_shapes=[pltpu.VMEM((4 * (fc // LANE), pad + tm, LANE), F32),
                        pltpu.VMEM((tm, d), F32)],
        compiler_params=pltpu.CompilerParams(
            dimension_semantics=("arbitrary", "arbitrary"), vmem_limit_bytes=VMEM_LIMIT),
        name="ffn",
    )(*args)
    return y, tail


def kernel(x_prompt, x_sample, c_prompt, c_sample, cache_k, cache_v, page_table, state_conv, state_ffn, ln_emb_g, ln_emb_b, w_ada, b_ada, w_in, lambda_q1, lambda_k1, lambda_q2, lambda_k2, subln_g, conv_w, conv_b, conv_ln_g, conv_ln_b, w_out, ln1_g, ln1_b, w_up, ffn_conv_w, ffn_conv_b, w_down, ln2_g, ln2_b):
    bsz, seq, d = x_prompt.shape
    n_seq, n_new, _ = x_sample.shape
    depth = w_ada.shape[0]
    assert depth == 1, "the prompt/sample activations are threaded for a single layer"
    n_phys, page = cache_k.shape[1], cache_k.shape[2]
    width = cache_k.shape[3] * cache_k.shape[4]
    d_ff = w_down.shape[1]
    conv_taps = conv_w.shape[1]
    ffn_taps = ffn_conv_w.shape[1]
    assert ffn_taps == 3
    alpha = (2 * depth) ** 0.25
    tm = 512
    n_rows_s = n_seq * n_new
    n_kh = width // HEAD_DIM
    n_vh = width // HEAD_LANES

    l = 0
    lambda_init = 0.8 - 0.6 * math.exp(-0.3 * l)
    lam_params = (lambda_q1[l], lambda_k1[l], lambda_q2[l], lambda_k2[l])
    w_in_bf = w_in[l].astype(BF16)
    w_out_bf = w_out[l].astype(BF16)
    w_up_bf = w_up[l].astype(BF16)
    w_down_bf = w_down[l].astype(BF16)

    c_rows = bsz + n_seq
    c_all = jnp.concatenate([c_prompt, c_sample], axis=0)
    c_all = jnp.pad(c_all, ((0, -c_rows % (2 * SUBLANE)), (0, 0)))
    mod = _modulation(c_all, w_ada[l], b_ada[l])
    mod_p = mod[:bsz].reshape(bsz, 1, 6 * d)
    mod_s = mod[bsz:c_rows]

    pp = _in_projection(x_prompt.reshape(bsz * seq, d), mod_p, ln_emb_g, ln_emb_b, w_in_bf,
                        groups=bsz, tm=tm,
                        names=("xp", "qt_b", "k_b", "kt", "v_heads", "vt_b", "u"))
    u = pp["u"]

    xs_tm = x_sample.swapaxes(0, 1).reshape(n_rows_s, d)
    ps = _in_projection(xs_tm, mod_s[None], ln_emb_g, ln_emb_b, w_in_bf, groups=n_new, tm=n_seq,
                        names=("xp", "q", "k", "kt", "v", "u"))
    qs, ks, kts, vs, us = ps["q"], ps["k"], ps["kt"], ps["v"], ps["u"]
    to_sm = lambda a: a.reshape(n_new, n_seq, -1).swapaxes(0, 1)
    pad8 = lambda a: jnp.pad(to_sm(a), ((0, 0), (0, SUBLANE - n_new), (0, 0)))
    cache_kt = cache_k[l].transpose(0, 2, 3, 1).reshape(n_phys, width, page)
    cache_vr = cache_v[l].reshape(n_phys, page * n_vh, HEAD_LANES)
    attn_s8, conv = _sample_attention_and_prompt_conv(
        pad8(qs), pad8(ks), pad8(vs), cache_kt, cache_vr, page_table, lam_params, subln_g[l],
        n_new, lambda_init, u, conv_w[l], conv_b[l], conv_ln_g[l], conv_ln_b[l], seq)
    attn_s = attn_s8[:, :n_new, :].astype(BF16).swapaxes(0, 1).reshape(n_rows_s, width)

    attn = _prompt_attention(pp["qt_b"], pp["k_b"], pp["vt_b"], lam_params, subln_g[l],
                             lambda_init)
    y_p, tail_p = _ffn(pp["xp"], attn, conv, mod_p, w_out_bf,
                       (ln1_g[l], ln1_b[l]), w_up_bf, ffn_conv_w[l], ffn_conv_b[l], w_down_bf,
                       (ln2_g[l], ln2_b[l]), groups=bsz, tiles=seq // tm, tm=tm,
                       pad=SUBLANE, tshift=1, state=None, alpha=alpha)
    y_prompt = y_p.reshape(bsz, seq, d)
    k_prompt = pp["kt"].reshape(bsz, n_kh, HEAD_DIM, seq).transpose(0, 3, 1, 2)[None]
    v_prompt = pp["v_heads"].reshape(1, bsz, seq, n_vh, HEAD_LANES)
    conv_prompt = u.reshape(bsz, seq, -1)[None, :, seq - (conv_taps - 1):, :]
    ffn_prompt = tail_p[None, :, SUBLANE - (ffn_taps - 1):, :]

    full_tm = jnp.concatenate([state_conv[l].swapaxes(0, 1), us.reshape(n_new, n_seq, -1)], axis=0)
    conv_s = _sample_conv(full_tm, conv_w[l], conv_b[l], conv_ln_g[l], conv_ln_b[l],
                          n_new).reshape(n_rows_s, -1)
    state_tm = state_ffn[l].swapaxes(0, 1).reshape((ffn_taps - 1) * n_seq, 2 * d_ff)
    y_s, tail_s = _ffn(ps["xp"], attn_s, conv_s, mod_s[None],
                       w_out_bf, (ln1_g[l], ln1_b[l]), w_up_bf,
                       ffn_conv_w[l], ffn_conv_b[l], w_down_bf, (ln2_g[l], ln2_b[l]),
                       groups=1, tiles=1, tm=n_rows_s, pad=(ffn_taps - 1) * n_seq,
                       tshift=n_seq, state=state_tm, alpha=alpha)
    y_sample = y_s.reshape(n_new, n_seq, d).swapaxes(0, 1)
    k_sample = kts.reshape(n_new, n_kh, HEAD_DIM, n_seq).transpose(3, 0, 1, 2)[None]
    v_sample = to_sm(vs).reshape(1, n_seq, n_new, n_vh, HEAD_LANES)
    conv_sample = full_tm[n_new:].swapaxes(0, 1)[None]
    ffn_sample = tail_s.reshape(ffn_taps - 1, n_seq, 2 * d_ff).swapaxes(0, 1)[None]

    return (y_prompt, y_sample, k_prompt, v_prompt, conv_prompt, ffn_prompt,
            k_sample, v_sample, conv_sample, ffn_sample)
```

```python
import functools
import math

import jax
import jax.numpy as jnp
from jax import lax
from jax.experimental import pallas as pl
from jax.experimental.pallas import tpu as pltpu

F32 = jnp.float32
BF16 = jnp.bfloat16

LN_EPS = 1e-5
HEAD_DIM = 64
HEAD_LANES = 2 * HEAD_DIM
LANE = 128
SUBLANE = 8
VMEM_LIMIT = 56 * 1024 * 1024
NEG_BIG = -1e30
ALIBI_POS_SPLIT = 64
PAGE_SLOTS = 4
PAGES_AHEAD = 2

def _ln(x, g, b):
    mu = jnp.mean(x, axis=-1, keepdims=True)
    xc = x - mu
    var = jnp.mean(xc * xc, axis=-1, keepdims=True)
    return xc * lax.rsqrt(var + LN_EPS) * g + b


def _silu(x):
    return x * jax.nn.sigmoid(x)


def _dot(a, b):
    return jnp.dot(a, b, preferred_element_type=F32)


def _const_spec(shape):
    nd = len(shape)
    return pl.BlockSpec(shape, lambda *_: (0,) * nd, pipeline_mode=pl.Buffered(1))


def _alibi_slope(h, n_heads):
    return 2.0 ** (-8.0 * (h + 1) / n_heads)


def _split_bf16(x):
    hi = x.astype(BF16)
    return hi, (x - hi.astype(F32)).astype(BF16)


def _mod_kernel(c_ref, w_ref, b_ref, o_ref):
    a_hi, a_lo = _split_bf16(_silu(c_ref[...]))
    w_hi, w_lo = _split_bf16(w_ref[...])
    o_ref[...] = _dot(a_hi, w_hi) + (_dot(a_hi, w_lo) + _dot(a_lo, w_hi)) + b_ref[...]


def _modulation(c_all, w_ada, b_ada):
    rows, d = c_all.shape
    n_out = w_ada.shape[1]
    return pl.pallas_call(
        _mod_kernel,
        grid=(n_out // d,),
        in_specs=[pl.BlockSpec((rows, d), lambda j: (0, 0)),
                  pl.BlockSpec((d, d), lambda j: (0, j)),
                  pl.BlockSpec((1, d), lambda j: (0, j))],
        out_specs=pl.BlockSpec((rows, d), lambda j: (0, j)),
        out_shape=jax.ShapeDtypeStruct((rows, n_out), F32),
        compiler_params=pltpu.CompilerParams(vmem_limit_bytes=VMEM_LIMIT),
        name="modulation",
    )(c_all, w_ada, b_ada.reshape(1, n_out))


def _inproj_kernel(x_ref, sh_ref, sc_ref, lg_ref, lb_ref, w_ref, *out_refs, width, names):
    out = dict(zip(names, out_refs))
    tm = x_ref.shape[0]
    xp = _ln(x_ref[...], lg_ref[...], lb_ref[...])
    h = (xp * (1.0 + sc_ref[...]) + sh_ref[...]).astype(BF16)

    def proj(i):
        return _dot(h, w_ref[:, i * width:(i + 1) * width])

    def emit(name, value):
        if name in out:
            out[name][...] = value.astype(out[name].dtype)

    emit("xp", xp)
    q = proj(0) * (HEAD_DIM ** -0.5)
    emit("q", q)
    if "qt_b" in out:
        emit("qt_b", q.T)
    k = proj(1)
    emit("k", k)
    emit("k_b", k)
    emit("kt", k.T)
    v = proj(2)
    emit("v", v)
    if "vt_b" in out:
        emit("vt_b", v.T)
    if "v_heads" in out:
        n_heads = width // HEAD_LANES
        for hd in range(n_heads):
            out["v_heads"][pl.ds(hd, tm, stride=n_heads), :] = (
                v[:, hd * HEAD_LANES:(hd + 1) * HEAD_LANES])
    emit("u", proj(3) * jax.nn.sigmoid(proj(4)))


def _in_projection(x2d, mod, ln_g, ln_b, w_in_bf, *, groups, tm, names):
    n, d = x2d.shape
    width = w_in_bf.shape[1] // 5
    n_heads = width // HEAD_LANES
    per_group = n // groups // tm
    mod_groups, r = mod.shape[0], mod.shape[1]

    def mod_spec(comp):
        return pl.BlockSpec((None, r, d),
                            lambda i: ((i // per_group) % mod_groups, 0, comp))

    row_spec = lambda rows, w: pl.BlockSpec((rows, w), lambda i: (i, 0))
    t_spec = pl.BlockSpec((None, width, tm), lambda i: (i // per_group, 0, i % per_group))
    t_shape = (groups, width, n // groups)
    dtypes = {"xp": F32, "q": BF16, "k": F32, "v": F32, "u": F32, "k_b": BF16, "v_heads": F32,
              "kt": F32, "qt_b": BF16, "vt_b": BF16}
    out_specs, out_shape = [], []
    for nm in names:
        if nm in ("kt", "qt_b", "vt_b"):
            spec, shape = t_spec, t_shape
        elif nm == "v_heads":
            spec, shape = row_spec(tm * n_heads, HEAD_LANES), (n * n_heads, HEAD_LANES)
        elif nm == "xp":
            spec, shape = row_spec(tm, d), (n, d)
        else:
            spec, shape = row_spec(tm, width), (n, width)
        out_specs.append(spec)
        out_shape.append(jax.ShapeDtypeStruct(shape, dtypes[nm]))
    outs = pl.pallas_call(
        functools.partial(_inproj_kernel, width=width, names=tuple(names)),
        grid=(n // tm,),
        in_specs=[row_spec(tm, d), mod_spec(0), mod_spec(1),
                  _const_spec((1, d)), _const_spec((1, d)),
                  _const_spec(w_in_bf.shape)],
        out_specs=out_specs,
        out_shape=out_shape,
        compiler_params=pltpu.CompilerParams(
            dimension_semantics=("arbitrary",), vmem_limit_bytes=VMEM_LIMIT),
        name="in_projection",
    )(x2d, mod, mod, ln_g.reshape(1, d), ln_b.reshape(1, d), w_in_bf)
    return dict(zip(names, outs))


def _lambda_value(lq1_ref, lk1_ref, lq2_ref, lk2_ref, lambda_init):
    s1 = jnp.sum(lq1_ref[...] * lk1_ref[...], axis=-1, keepdims=True)
    s2 = jnp.sum(lq2_ref[...] * lk2_ref[...], axis=-1, keepdims=True)
    return jnp.exp(s1) - jnp.exp(s2) + lambda_init


def _sub_ln(o, g, lambda_init):
    ms = jnp.mean(o * o, axis=-1, keepdims=True)
    return o * lax.rsqrt(ms + LN_EPS) * g * (1.0 - lambda_init)


def _prompt_attn_kernel(qi_tab, ki_tab, qt_ref, k_ref, vt_ref, lq1_ref, lk1_ref, lq2_ref, lk2_ref,
                        g_ref, o_ref, qaug_scr, pos_scr, s_scr, m_scr, acc_scr, *,
                        n_heads, tq, tk, lambda_init):
    pair = pl.program_id(1)
    qi = qi_tab[pair]
    ki = ki_tab[pair]

    @pl.when(ki == 0)
    def _init():
        m_scr[...] = jnp.full(m_scr.shape, NEG_BIG, F32)
        acc_scr[...] = jnp.zeros(acc_scr.shape, F32)
        r = lax.broadcasted_iota(jnp.int32, (HEAD_LANES, tq), 0)
        for h in range(n_heads):
            qt = qt_ref[h * HEAD_LANES:(h + 1) * HEAD_LANES, :].astype(F32)
            slope = _alibi_slope(h, n_heads)
            bot = jnp.where(r == 0, ALIBI_POS_SPLIT * slope, jnp.where(r == 1, slope, 0.0))
            for j in range(2):
                own = (r < HEAD_DIM) if j == 0 else (r >= HEAD_DIM)
                top = jnp.where(own, qt, 0.0)
                qaug_scr[2 * h + j] = jnp.concatenate([top, bot], axis=0).astype(BF16)

    def step(on_diagonal):
        kr = lax.broadcasted_iota(jnp.int32, (tk, LANE), 0)
        lane = lax.broadcasted_iota(jnp.int32, (tk, LANE), 1)
        kpos = ki * tk + kr
        pos_hi = kpos // ALIBI_POS_SPLIT
        pos_lo = kpos % ALIBI_POS_SPLIT
        pos_scr[...] = jnp.where(lane == 0, pos_hi, jnp.where(lane == 1, pos_lo, 0)
                                 ).astype(F32).astype(BF16)
        ones_row = jnp.where(lax.broadcasted_iota(jnp.int32, (2 * SUBLANE, tk), 0) == 0,
                             1.0, 0.0).astype(BF16)
        if on_diagonal:
            key = lax.broadcasted_iota(jnp.int32, (tk, tq), 0)
            qry = lax.broadcasted_iota(jnp.int32, (tk, tq), 1)
            visible = key <= qry

        def head_cols(slab):
            return slice((slab // 2) * HEAD_LANES, (slab // 2 + 1) * HEAD_LANES)

        def scores(slab):
            kaug = jnp.concatenate([k_ref[:, head_cols(slab)], pos_scr[...]], axis=1)
            s_scr[slab % 2] = _dot(kaug, qaug_scr[slab])

        def softmax_update(slab):
            s = s_scr[slab % 2]
            if on_diagonal:
                s = jnp.where(visible, s, NEG_BIG)
            m_prev = m_scr[slab]
            m_new = jnp.maximum(m_prev, jnp.max(s, axis=0, keepdims=True))
            alpha = jnp.exp(m_prev - m_new)
            p = jnp.exp(s - m_new).astype(BF16)
            vaug = jnp.concatenate([vt_ref[head_cols(slab), :], ones_row], axis=0)
            acc_scr[slab] = alpha * acc_scr[slab] + _dot(vaug, p)
            m_scr[slab] = m_new

        n_slabs = 2 * n_heads
        scores(0)
        for slab in range(n_slabs):
            if slab + 1 < n_slabs:
                scores(slab + 1)
            softmax_update(slab)

    @pl.when(ki < qi)
    def _below_diagonal():
        step(False)

    @pl.when(ki == qi)
    def _diagonal_and_finalize():
        step(True)
        lam = _lambda_value(lq1_ref, lk1_ref, lq2_ref, lk2_ref, lambda_init)
        def normalised(slab):
            acc = acc_scr[slab]
            return acc[:HEAD_LANES] * (1.0 / acc[HEAD_LANES:HEAD_LANES + 1])

        for h in range(n_heads):
            o = normalised(2 * h) - lam * normalised(2 * h + 1)
            ms = jnp.mean(o * o, axis=0, keepdims=True)
            o = o * lax.rsqrt(ms + LN_EPS) * g_ref[...] * (1.0 - lambda_init)
            o_ref[:, h * HEAD_LANES:(h + 1) * HEAD_LANES] = o.T.astype(o_ref.dtype)


def _prompt_attention(qt_b, k_b, vt_b, lam_params, subln_g, lambda_init, tq=512):
    bsz, width, seq = qt_b.shape
    n_heads = width // HEAD_LANES
    tk = tq
    nq = seq // tq
    assert (seq - 1) // ALIBI_POS_SPLIT < 256, "key position parts must be exact in bf16"
    lq1, lk1, lq2, lk2 = [p.reshape(1, HEAD_DIM) for p in lam_params]
    pairs = [(i, j) for i in range(nq) for j in range(i + 1)]
    qi_tab = jnp.asarray([p[0] for p in pairs], jnp.int32)
    ki_tab = jnp.asarray([p[1] for p in pairs], jnp.int32)
    qt_spec = pl.BlockSpec((None, width, tq), lambda b, s, qi, ki: (b, 0, qi[s]))
    k_spec = pl.BlockSpec((tk, width), lambda b, s, qi, ki: (b * nq + ki[s], 0))
    vt_spec = pl.BlockSpec((None, width, tk), lambda b, s, qi, ki: (b, 0, ki[s]))
    o_spec = pl.BlockSpec((tq, width), lambda b, s, qi, ki: (b * nq + qi[s], 0))
    small = lambda r, w: pl.BlockSpec((r, w), lambda b, s, qi, ki: (0, 0))
    grid_spec = pltpu.PrefetchScalarGridSpec(
        num_scalar_prefetch=2,
        grid=(bsz, len(pairs)),
        in_specs=[qt_spec, k_spec, vt_spec,
                  small(1, HEAD_DIM), small(1, HEAD_DIM), small(1, HEAD_DIM), small(1, HEAD_DIM),
                  small(HEAD_LANES, 1)],
        out_specs=o_spec,
        scratch_shapes=[pltpu.VMEM((2 * n_heads, 2 * HEAD_LANES, tq), BF16),
                        pltpu.VMEM((tk, LANE), BF16),
                        pltpu.VMEM((2, tk, tq), F32),
                        pltpu.VMEM((2 * n_heads, 1, tq), F32),
                        pltpu.VMEM((2 * n_heads, HEAD_LANES + 2 * SUBLANE, tq), F32)],
    )
    return pl.pallas_call(
        functools.partial(_prompt_attn_kernel, n_heads=n_heads, tq=tq, tk=tk,
                          lambda_init=lambda_init),
        grid_spec=grid_spec,
        out_shape=jax.ShapeDtypeStruct((bsz * seq, width), BF16),
        compiler_params=pltpu.CompilerParams(
            dimension_semantics=("arbitrary", "arbitrary"), vmem_limit_bytes=VMEM_LIMIT),
        name="prompt_attention",
    )(qi_tab, ki_tab, qt_b, k_b, vt_b, lq1, lk1, lq2, lk2, subln_g.reshape(HEAD_LANES, 1))


def _sample_attn_tile(q_ref, kn_ref, vn_ref, lq1_ref, lk1_ref, lq2_ref, lk2_ref, g_ref,
                      kt_refs, v_refs, o_ref, s_scr, *, page, n_new, n_heads, lambda_init):
    n_pages = len(kt_refs)
    width = n_heads * HEAD_LANES
    n_maps = 2 * n_heads
    rows = n_new * n_maps
    past = n_pages * page

    q = q_ref[0].astype(F32)
    mp = lax.broadcasted_iota(jnp.int32, (n_maps, width), 0)
    cl = lax.broadcasted_iota(jnp.int32, (n_maps, width), 1)
    map_lanes = (cl // HEAD_DIM) == mp
    qbd32 = jnp.concatenate(
        [jnp.where(map_lanes, jnp.broadcast_to(q[i:i + 1, :], (n_maps, width)), 0.0)
         for i in range(n_new)], axis=0)
    qbd = qbd32.astype(BF16)

    for p in range(n_pages):
        s_scr[:, p * page:(p + 1) * page] = _dot(qbd, kt_refs[p][...].astype(BF16))

    rmap = lax.broadcasted_iota(jnp.int32, (rows, 1), 0) % n_maps
    rqry = lax.broadcasted_iota(jnp.int32, (rows, 1), 0) // n_maps
    slope = jnp.zeros((rows, 1), F32)
    for h in range(n_heads):
        slope = jnp.where(rmap // 2 == h, _alibi_slope(h, n_heads), slope)
    tpos = lax.broadcasted_iota(jnp.int32, (1, past), 1).astype(F32)
    s = s_scr[...] + slope * tpos

    kn = kn_ref[0]
    vn = vn_ref[0]
    s_new = []
    for j in range(n_new):
        sj = jnp.sum(qbd32 * kn[j:j + 1, :], axis=-1, keepdims=True) + slope * float(past + j)
        s_new.append(jnp.where(rqry >= j, sj, NEG_BIG))
    m = jnp.max(s, axis=-1, keepdims=True)
    for sj in s_new:
        m = jnp.maximum(m, sj)
    pr = jnp.exp(s - m)
    l = jnp.sum(pr, axis=-1, keepdims=True)
    prb = pr.astype(BF16)
    p_new = []
    for sj in s_new:
        pj = jnp.exp(sj - m)
        l = l + pj
        p_new.append(pj)
    inv_l = 1.0 / l

    lam = _lambda_value(lq1_ref, lk1_ref, lq2_ref, lk2_ref, lambda_init)
    mrow = lax.broadcasted_iota(jnp.int32, (n_maps, 1), 0)
    for h in range(n_heads):
        cols = slice(h * HEAD_LANES, (h + 1) * HEAD_LANES)
        out = jnp.zeros((rows, HEAD_LANES), F32)
        for j, pj in enumerate(p_new):
            out = out + pj * vn[j:j + 1, cols]
        for p in range(n_pages):
            vh = v_refs[p][pl.ds(h, page, stride=n_heads), :].astype(BF16)
            out = out + _dot(prb[:, p * page:(p + 1) * page], vh)
        out = out * inv_l
        weight = jnp.where(mrow == 2 * h, 1.0, jnp.where(mrow == 2 * h + 1, -lam, 0.0))
        for i in range(n_new):
            o = jnp.sum(out[i * n_maps:(i + 1) * n_maps, :] * weight, axis=0, keepdims=True)
            o_ref[0, i:i + 1, cols] = _sub_ln(o, g_ref[...], lambda_init)
    o_ref[0, n_new:, :] = jnp.zeros((SUBLANE - n_new, width), F32)


def _sample_attn_prompt_conv_kernel(pt_ref, q_ref, kn_ref, vn_ref, lq1_ref, lk1_ref, lq2_ref,
                                    lk2_ref, g_ref, u_ref, cw_ref, cb_ref, clg_ref, clb_ref,
                                    kt_hbm, v_hbm, o_ref, conv_o_ref,
                                    s_scr, buf, wrep, cv_scr, kbuf, vbuf, sem, *,
                                    n_seq, n_pages, page, n_new, n_heads, lambda_init,
                                    tiles_per_seq, tt, taps, halo):
    step = pl.program_id(0)

    def page_copies(seq, slot, pages):
        copies = []
        for p in pages:
            page_id = pt_ref[seq * n_pages + p]
            copies.append(pltpu.make_async_copy(kt_hbm.at[page_id], kbuf.at[slot, p],
                                                sem.at[slot, 0, p]))
            copies.append(pltpu.make_async_copy(v_hbm.at[page_id], vbuf.at[slot, p],
                                                sem.at[slot, 1, p]))
        return copies

    all_pages = range(n_pages)

    @pl.when(step == 0)
    def _prime():
        for ahead in range(PAGES_AHEAD):
            for copy in page_copies(ahead, ahead, all_pages):
                copy.start()

    for slot in range(PAGE_SLOTS):
        seq = step * PAGE_SLOTS + slot
        next_slot = (slot + PAGES_AHEAD) % PAGE_SLOTS
        next_seq = seq + PAGES_AHEAD
        next_seq = jnp.where(next_seq >= n_seq, next_seq - n_seq, next_seq)

        def fetch_next(i, n_blocks, next_seq=next_seq, next_slot=next_slot):
            per_block = -(-n_pages // n_blocks)
            pages = range(i * per_block, min((i + 1) * per_block, n_pages))
            for copy in page_copies(next_seq, next_slot, pages):
                copy.start()

        rows = pl.ds(slot * tt, tt)
        _conv_tile(seq % tiles_per_seq == 0, u_ref.at[rows], cw_ref, cb_ref, clg_ref, clb_ref,
                   conv_o_ref.at[rows], buf, wrep, cv_scr, tt=tt, taps=taps, halo=halo, rc=tt,
                   before_block=fetch_next)
        for copy in page_copies(seq, slot, all_pages):
            copy.wait()
        one = pl.ds(slot, 1)
        _sample_attn_tile(q_ref.at[one], kn_ref.at[one], vn_ref.at[one], lq1_ref, lk1_ref,
                          lq2_ref, lk2_ref, g_ref,
                          [kbuf.at[slot, p] for p in all_pages],
                          [vbuf.at[slot, p] for p in all_pages],
                          o_ref.at[one], s_scr, page=page, n_new=n_new, n_heads=n_heads,
                          lambda_init=lambda_init)

    @pl.when(step == pl.num_programs(0) - 1)
    def _drain():
        for ahead in range(PAGES_AHEAD):
            for copy in page_copies(ahead, ahead, all_pages):
                copy.wait()


def _sample_attention_and_prompt_conv(q8, kn8, vn8, cache_kt, cache_v, page_table, lam_params,
                                      subln_g, n_new, lambda_init,
                                      u, conv_w, conv_b, conv_ln_g, conv_ln_b, seq):
    n_seq, _, width = q8.shape
    n_pages = page_table.shape[1]
    page = cache_kt.shape[2]
    n_heads = width // HEAD_LANES
    assert 2 * n_heads == SUBLANE and n_new <= SUBLANE and n_seq % PAGE_SLOTS == 0
    n_rows, conv_width = u.shape
    taps = conv_w.shape[0]
    halo = -(-(taps - 1) // SUBLANE) * SUBLANE
    tt = n_rows // n_seq
    assert tt * n_seq == n_rows and seq % tt == 0 and tt >= halo and tt % SUBLANE == 0
    n_slab = conv_width // LANE
    lq1, lk1, lq2, lk2 = [p.reshape(1, HEAD_DIM) for p in lam_params]
    seq_spec = pl.BlockSpec((PAGE_SLOTS, SUBLANE, width), lambda s, pt: (s, 0, 0))
    small = lambda r, w: pl.BlockSpec((r, w), lambda s, pt: (0, 0))
    conv_rows = pl.BlockSpec((PAGE_SLOTS * tt, conv_width), lambda s, pt: (s, 0))
    hbm = pl.BlockSpec(memory_space=pl.ANY)

    grid_spec = pltpu.PrefetchScalarGridSpec(
        num_scalar_prefetch=1,
        grid=(n_seq // PAGE_SLOTS,),
        in_specs=[seq_spec, seq_spec, seq_spec,
                  small(1, HEAD_DIM), small(1, HEAD_DIM), small(1, HEAD_DIM), small(1, HEAD_DIM),
                  small(1, HEAD_LANES),
                  conv_rows, small(taps, conv_width), small(1, conv_width),
                  small(1, conv_width), small(1, conv_width), hbm, hbm],
        out_specs=[seq_spec, conv_rows],
        scratch_shapes=[pltpu.VMEM((n_new * 2 * n_heads, n_pages * page), F32),
                        pltpu.VMEM((n_slab, halo + tt, LANE), F32),
                        pltpu.VMEM((taps + 1, n_slab, SUBLANE, LANE), F32),
                        pltpu.VMEM((n_slab, tt, LANE), F32),
                        pltpu.VMEM((PAGE_SLOTS, n_pages) + cache_kt.shape[1:], F32),
                        pltpu.VMEM((PAGE_SLOTS, n_pages) + cache_v.shape[1:], F32),
                        pltpu.SemaphoreType.DMA((PAGE_SLOTS, 2, n_pages))],
    )
    vec = lambda a: a.reshape(1, -1)
    return pl.pallas_call(
        functools.partial(_sample_attn_prompt_conv_kernel, n_seq=n_seq, n_pages=n_pages,
                          page=page, n_new=n_new, n_heads=n_heads, lambda_init=lambda_init,
                          tiles_per_seq=seq // tt, tt=tt, taps=taps, halo=halo),
        grid_spec=grid_spec,
        out_shape=[jax.ShapeDtypeStruct((n_seq, SUBLANE, width), F32),
                   jax.ShapeDtypeStruct((n_rows, conv_width), BF16)],
        compiler_params=pltpu.CompilerParams(
            dimension_semantics=("arbitrary",), vmem_limit_bytes=VMEM_LIMIT),
        name="sample_attention_prompt_conv",
    )(page_table.reshape(-1), q8, kn8, vn8, lq1, lk1, lq2, lk2, vec(subln_g),
      u, conv_w, vec(conv_b), vec(conv_ln_g), vec(conv_ln_b), cache_kt, cache_v)


def _conv_tile(first, u_ref, w_ref, b_ref, lg_ref, lb_ref, o_ref, buf, wrep, cv_scr, *,
               tt, taps, halo, rc, before_block):
    n_slab = buf.shape[0]

    @pl.when(first)
    def _start_of_sequence():
        buf[:, 0:halo, :] = jnp.zeros((n_slab, halo, LANE), F32)
        for s in range(n_slab):
            cols = slice(s * LANE, (s + 1) * LANE)
            wrep[taps, s] = jnp.broadcast_to(b_ref[:, cols], (SUBLANE, LANE))
            for j in range(taps):
                wrep[j, s] = jnp.broadcast_to(w_ref[j:j + 1, cols], (SUBLANE, LANE))

    for s in range(n_slab):
        buf[s, halo:halo + tt, :] = u_ref[:, s * LANE:(s + 1) * LANE]
    base = halo - (taps - 1)
    groups = rc // SUBLANE

    def block_conv(i):
        s = i % n_slab
        r0 = (i // n_slab) * rc
        acc = jnp.broadcast_to(wrep[taps, s][None], (groups, SUBLANE, LANE))
        for phase in range(SUBLANE):
            phase_taps = range(phase, taps, SUBLANE)
            n_groups = groups + len(phase_taps) - 1
            x = buf[s, pl.ds(base + r0 + phase, n_groups * SUBLANE), :].reshape(
                n_groups, SUBLANE, LANE)
            for a, j in enumerate(phase_taps):
                acc = acc + wrep[j, s][None] * x[a:a + groups]
        cv_scr[s, pl.ds(r0, rc), :] = acc.reshape(rc, LANE)

    n_blocks = (tt // rc) * n_slab
    for i in range(n_blocks):
        before_block(i, n_blocks)
        block_conv(i)
    cv = jnp.concatenate([cv_scr[s] for s in range(n_slab)], axis=1)
    o_ref[...] = _silu(_ln(cv, lg_ref[...], lb_ref[...])).astype(o_ref.dtype)
    for s in range(n_slab):
        buf[s, 0:halo, :] = buf[s, tt:tt + halo, :]


def _sample_conv_kernel(full_ref, w_ref, b_ref, lg_ref, lb_ref, o_ref, *, n_new, n_seq, taps):
    for t in range(n_new):
        cv = jnp.broadcast_to(b_ref[...], (n_seq, full_ref.shape[2]))
        for j in range(taps):
            cv = cv + w_ref[j:j + 1, :] * full_ref[t + j]
        o_ref[t] = _silu(_ln(cv, lg_ref[...], lb_ref[...])).astype(o_ref.dtype)


def _sample_conv(full_tm, conv_w, conv_b, ln_g, ln_b, n_new):
    n_time, n_seq, width = full_tm.shape
    taps = conv_w.shape[0]
    return pl.pallas_call(
        functools.partial(_sample_conv_kernel, n_new=n_new, n_seq=n_seq, taps=taps),
        out_shape=jax.ShapeDtypeStruct((n_new, n_seq, width), BF16),
        compiler_params=pltpu.CompilerParams(vmem_limit_bytes=VMEM_LIMIT),
        name="sample_conv",
    )(full_tm, conv_w, conv_b.reshape(1, width), ln_g.reshape(1, width), ln_b.reshape(1, width))


def _ffn_kernel(*refs, has_state, tm, pad, tshift, fc, d_ff, alpha):
    (xp_ref, attn_ref, conv_ref, g1_ref, sh2_ref, sc2_ref, g2_ref,
     wout_ref, l1g_ref, l1b_ref, wup_ref, cw_ref, cb_ref, wdn_ref, l2g_ref, l2b_ref) = refs[:16]
    refs = refs[16:]
    if has_state:
        state_ref, refs = refs[0], refs[1:]
    y_ref, tail_ref, ubuf, acc_ref = refs
    half = attn_ref.shape[1]
    n_slab = fc // LANE

    if has_state:
        hist_ref = state_ref
    else:
        hist_ref = tail_ref

        @pl.when(pl.program_id(1) == 0)
        def _zero_history():
            tail_ref[...] = jnp.zeros(tail_ref.shape, F32)

    def mod_rows(ref):
        m = ref[...]
        reps = 1 if m.shape[0] == 1 else tm // m.shape[0]
        return m if reps == 1 else jnp.concatenate([m] * reps, axis=0)

    mix = _dot(attn_ref[...], wout_ref[0:half, :]) + _dot(conv_ref[...], wout_ref[half:, :])
    x1 = _ln(alpha * xp_ref[...] + mod_rows(g1_ref) * mix, l1g_ref[...], l1b_ref[...])
    h2 = (x1 * (1.0 + mod_rows(sc2_ref)) + mod_rows(sh2_ref)).astype(BF16)

    n_chunks = d_ff // fc

    def up_project(c):
        for part in range(2):
            col = part * d_ff + c * fc
            up = _dot(h2, wup_ref[:, col:col + fc])
            for s in range(n_slab):
                cols = slice(col + s * LANE, col + (s + 1) * LANE)
                slab = ((c % 2) * 2 + part) * n_slab + s
                ubuf[slab, 0:pad, :] = hist_ref[:, cols]
                ubuf[slab, pad:pad + tm, :] = up[:, s * LANE:(s + 1) * LANE]
            tail_ref[:, col:col + fc] = up[tm - pad:, :]

    def conv_half(c, part):
        parts = []
        for s in range(n_slab):
            col = part * d_ff + c * fc + s * LANE
            cols = slice(col, col + LANE)
            slab = ((c % 2) * 2 + part) * n_slab + s
            parts.append(cw_ref[0:1, cols] * ubuf[slab, pad - 2 * tshift:pad - 2 * tshift + tm, :]
                         + cw_ref[1:2, cols] * ubuf[slab, pad - tshift:pad - tshift + tm, :]
                         + cw_ref[2:3, cols] * ubuf[slab, pad:pad + tm, :] + cb_ref[:, cols])
        return jnp.concatenate(parts, axis=1)

    up_project(0)
    for c in range(n_chunks):
        if c + 1 < n_chunks:
            up_project(c + 1)
        g = (_silu(conv_half(c, 0)) * conv_half(c, 1)).astype(BF16)
        contrib = _dot(g, wdn_ref[c * fc:(c + 1) * fc, :])
        if c == 0:
            acc_ref[...] = contrib
        else:
            acc_ref[...] += contrib

    y_ref[...] = _ln(alpha * x1 + mod_rows(g2_ref) * acc_ref[...], l2g_ref[...], l2b_ref[...])


def _ffn(xp2d, attn, conv, mod, w_out_bf, ln1, w_up_bf, ffn_conv_w, ffn_conv_b,
         w_down_bf, ln2, *, groups, tiles, tm, pad, tshift, state, alpha, fc=256):
    n, d = xp2d.shape
    half = attn.shape[1]
    d_ff = w_down_bf.shape[0]
    r = mod.shape[1]
    has_state = state is not None
    assert pad == 2 * tshift or not has_state
    assert tm >= pad
    row = lambda w: pl.BlockSpec((tm, w), lambda b, t: (b * tiles + t, 0))
    mod_spec = lambda comp: pl.BlockSpec((None, r, d), lambda b, t: (b, 0, comp))
    vec = lambda a: a.reshape(1, -1)
    args = [xp2d, attn, conv, mod, mod, mod, mod,
            w_out_bf, vec(ln1[0]), vec(ln1[1]), w_up_bf, ffn_conv_w, vec(ffn_conv_b),
            w_down_bf, vec(ln2[0]), vec(ln2[1])]
    in_specs = [row(d), row(half), row(half), mod_spec(2), mod_spec(3), mod_spec(4), mod_spec(5),
                _const_spec(w_out_bf.shape), _const_spec((1, d)), _const_spec((1, d)),
                _const_spec(w_up_bf.shape), _const_spec(ffn_conv_w.shape),
                _const_spec((1, 2 * d_ff)), _const_spec(w_down_bf.shape),
                _const_spec((1, d)), _const_spec((1, d))]
    if has_state:
        args.append(state)
        in_specs.append(_const_spec(state.shape))
    y, tail = pl.pallas_call(
        functools.partial(_ffn_kernel, has_state=has_state, tm=tm, pad=pad,
                          tshift=tshift, fc=fc, d_ff=d_ff, alpha=alpha),
        grid=(groups, tiles),
        in_specs=in_specs,
        out_specs=[row(d), pl.BlockSpec((None, pad, 2 * d_ff), lambda b, t: (b, 0, 0))],
        out_shape=[jax.ShapeDtypeStruct((n, d), F32),
                   jax.ShapeDtypeStruct((groups, pad, 2 * d_ff), F32)],
        scratch_shapes=[pltpu.VMEM((4 * (fc // LANE), pad + tm, LANE), F32),
                        pltpu.VMEM((tm, d), F32)],
        compiler_params=pltpu.CompilerParams(
            dimension_semantics=("arbitrary", "arbitrary"), vmem_limit_bytes=VMEM_LIMIT),
        name="ffn",
    )(*args)
    return y, tail


def kernel(x_prompt, x_sample, c_prompt, c_sample, cache_k, cache_v, page_table, state_conv, state_ffn, ln_emb_g, ln_emb_b, w_ada, b_ada, w_in, lambda_q1, lambda_k1, lambda_q2, lambda_k2, subln_g, conv_w, conv_b, conv_ln_g, conv_ln_b, w_out, ln1_g, ln1_b, w_up, ffn_conv_w, ffn_conv_b, w_down, ln2_g, ln2_b):
    bsz, seq, d = x_prompt.shape
    n_seq, n_new, _ = x_sample.shape
    depth = w_ada.shape[0]
    assert depth == 1, "the prompt/sample activations are threaded for a single layer"
    n_phys, page = cache_k.shape[1], cache_k.shape[2]
    width = cache_k.shape[3] * cache_k.shape[4]
    d_ff = w_down.shape[1]
    conv_taps = conv_w.shape[1]
    ffn_taps = ffn_conv_w.shape[1]
    assert ffn_taps == 3
    alpha = (2 * depth) ** 0.25
    tm = 512
    n_rows_s = n_seq * n_new
    n_kh = width // HEAD_DIM
    n_vh = width // HEAD_LANES

    l = 0
    lambda_init = 0.8 - 0.6 * math.exp(-0.3 * l)
    lam_params = (lambda_q1[l], lambda_k1[l], lambda_q2[l], lambda_k2[l])
    w_in_bf = w_in[l].astype(BF16)
    w_out_bf = w_out[l].astype(BF16)
    w_up_bf = w_up[l].astype(BF16)
    w_down_bf = w_down[l].astype(BF16)

    c_rows = bsz + n_seq
    c_all = jnp.concatenate([c_prompt, c_sample], axis=0)
    c_all = jnp.pad(c_all, ((0, -c_rows % (2 * SUBLANE)), (0, 0)))
    mod = _modulation(c_all, w_ada[l], b_ada[l])
    mod_p = mod[:bsz].reshape(bsz, 1, 6 * d)
    mod_s = mod[bsz:c_rows]

    pp = _in_projection(x_prompt.reshape(bsz * seq, d), mod_p, ln_emb_g, ln_emb_b, w_in_bf,
                        groups=bsz, tm=tm,
                        names=("xp", "qt_b", "k_b", "kt", "v_heads", "vt_b", "u"))
    u = pp["u"]

    xs_tm = x_sample.swapaxes(0, 1).reshape(n_rows_s, d)
    ps = _in_projection(xs_tm, mod_s[None], ln_emb_g, ln_emb_b, w_in_bf, groups=n_new, tm=n_seq,
                        names=("xp", "q", "k", "kt", "v", "u"))
    qs, ks, kts, vs, us = ps["q"], ps["k"], ps["kt"], ps["v"], ps["u"]
    to_sm = lambda a: a.reshape(n_new, n_seq, -1).swapaxes(0, 1)
    pad8 = lambda a: jnp.pad(to_sm(a), ((0, 0), (0, SUBLANE - n_new), (0, 0)))
    cache_kt = cache_k[l].transpose(0, 2, 3, 1).reshape(n_phys, width, page)
    cache_vr = cache_v[l].reshape(n_phys, page * n_vh, HEAD_LANES)
    attn_s8, conv = _sample_attention_and_prompt_conv(
        pad8(qs), pad8(ks), pad8(vs), cache_kt, cache_vr, page_table, lam_params, subln_g[l],
        n_new, lambda_init, u, conv_w[l], conv_b[l], conv_ln_g[l], conv_ln_b[l], seq)
    attn_s = attn_s8[:, :n_new, :].astype(BF16).swapaxes(0, 1).reshape(n_rows_s, width)

    attn = _prompt_attention(pp["qt_b"], pp["k_b"], pp["vt_b"], lam_params, subln_g[l],
                             lambda_init)
    y_p, tail_p = _ffn(pp["xp"], attn, conv, mod_p, w_out_bf,
                       (ln1_g[l], ln1_b[l]), w_up_bf, ffn_conv_w[l], ffn_conv_b[l], w_down_bf,
                       (ln2_g[l], ln2_b[l]), groups=bsz, tiles=seq // tm, tm=tm,
                       pad=SUBLANE, tshift=1, state=None, alpha=alpha)
    y_prompt = y_p.reshape(bsz, seq, d)
    k_prompt = pp["kt"].reshape(bsz, n_kh, HEAD_DIM, seq).transpose(0, 3, 1, 2)[None]
    v_prompt = pp["v_heads"].reshape(1, bsz, seq, n_vh, HEAD_LANES)
    conv_prompt = u.reshape(bsz, seq, -1)[None, :, seq - (conv_taps - 1):, :]
    ffn_prompt = tail_p[None, :, SUBLANE - (ffn_taps - 1):, :]

    full_tm = jnp.concatenate([state_conv[l].swapaxes(0, 1), us.reshape(n_new, n_seq, -1)], axis=0)
    conv_s = _sample_conv(full_tm, conv_w[l], conv_b[l], conv_ln_g[l], conv_ln_b[l],
                          n_new).reshape(n_rows_s, -1)
    state_tm = state_ffn[l].swapaxes(0, 1).reshape((ffn_taps - 1) * n_seq, 2 * d_ff)
    y_s, tail_s = _ffn(ps["xp"], attn_s, conv_s, mod_s[None],
                       w_out_bf, (ln1_g[l], ln1_b[l]), w_up_bf,
                       ffn_conv_w[l], ffn_conv_b[l], w_down_bf, (ln2_g[l], ln2_b[l]),
                       groups=1, tiles=1, tm=n_rows_s, pad=(ffn_taps - 1) * n_seq,
                       tshift=n_seq, state=state_tm, alpha=alpha)
    y_sample = y_s.reshape(n_new, n_seq, d).swapaxes(0, 1)
    k_sample = kts.reshape(n_new, n_kh, HEAD_DIM, n_seq).transpose(3, 0, 1, 2)[None]
    v_sample = to_sm(vs).reshape(1, n_seq, n_new, n_vh, HEAD_LANES)
    conv_sample = full_tm[n_new:].swapaxes(0, 1)[None]
    ffn_sample = tail_s.reshape(ffn_taps - 1, n_seq, 2 * d_ff).swapaxes(0, 1)[None]

    return (y_prompt, y_sample, k_prompt, v_prompt, conv_prompt, ffn_prompt,
            k_sample, v_sample, conv_sample, ffn_sample)
```

```python
import functools
import math

import jax
import jax.numpy as jnp
from jax import lax
from jax.experimental import pallas as pl
from jax.experimental.pallas import tpu as pltpu

F32 = jnp.float32
BF16 = jnp.bfloat16

LN_EPS = 1e-5
HEAD_DIM = 64
HEAD_LANES = 2 * HEAD_DIM
LANE = 128
SUBLANE = 8
VMEM_LIMIT = 56 * 1024 * 1024
NEG_BIG = -1e30
ALIBI_POS_SPLIT = 64
PAGE_SLOTS = 4
PAGES_AHEAD = 3

def _ln(x, g, b):
    mu = jnp.mean(x, axis=-1, keepdims=True)
    xc = x - mu
    var = jnp.mean(xc * xc, axis=-1, keepdims=True)
    return xc * lax.rsqrt(var + LN_EPS) * g + b


def _silu(x):
    return x * jax.nn.sigmoid(x)


def _dot(a, b):
    return jnp.dot(a, b, preferred_element_type=F32)


def _const_spec(shape):
    nd = len(shape)
    return pl.BlockSpec(shape, lambda *_: (0,) * nd, pipeline_mode=pl.Buffered(1))


def _alibi_slope(h, n_heads):
    return 2.0 ** (-8.0 * (h + 1) / n_heads)


def _split_bf16(x):
    hi = x.astype(BF16)
    return hi, (x - hi.astype(F32)).astype(BF16)


def _mod_kernel(c_ref, w_ref, b_ref, o_ref):
    a_hi, a_lo = _split_bf16(_silu(c_ref[...]))
    w_hi, w_lo = _split_bf16(w_ref[...])
    o_ref[...] = _dot(a_hi, w_hi) + (_dot(a_hi, w_lo) + _dot(a_lo, w_hi)) + b_ref[...]


def _modulation(c_all, w_ada, b_ada):
    rows, d = c_all.shape
    n_out = w_ada.shape[1]
    return pl.pallas_call(
        _mod_kernel,
        grid=(n_out // d,),
        in_specs=[pl.BlockSpec((rows, d), lambda j: (0, 0)),
                  pl.BlockSpec((d, d), lambda j: (0, j)),
                  pl.BlockSpec((1, d), lambda j: (0, j))],
        out_specs=pl.BlockSpec((rows, d), lambda j: (0, j)),
        out_shape=jax.ShapeDtypeStruct((rows, n_out), F32),
        compiler_params=pltpu.CompilerParams(vmem_limit_bytes=VMEM_LIMIT),
        name="modulation",
    )(c_all, w_ada, b_ada.reshape(1, n_out))


def _inproj_kernel(x_ref, sh_ref, sc_ref, lg_ref, lb_ref, w_ref, *out_refs, width, names):
    out = dict(zip(names, out_refs))
    tm = x_ref.shape[0]
    xp = _ln(x_ref[...], lg_ref[...], lb_ref[...])
    h = (xp * (1.0 + sc_ref[...]) + sh_ref[...]).astype(BF16)

    def proj(i):
        return _dot(h, w_ref[:, i * width:(i + 1) * width])

    def emit(name, value):
        if name in out:
            out[name][...] = value.astype(out[name].dtype)

    emit("xp", xp)
    q = proj(0) * (HEAD_DIM ** -0.5)
    emit("q", q)
    if "qt_b" in out:
        emit("qt_b", q.T)
    k = proj(1)
    emit("k", k)
    emit("k_b", k)
    emit("kt", k.T)
    v = proj(2)
    emit("v", v)
    if "vt_b" in out:
        emit("vt_b", v.T)
    if "v_heads" in out:
        n_heads = width // HEAD_LANES
        for hd in range(n_heads):
            out["v_heads"][pl.ds(hd, tm, stride=n_heads), :] = (
                v[:, hd * HEAD_LANES:(hd + 1) * HEAD_LANES])
    emit("u", proj(3) * jax.nn.sigmoid(proj(4)))


def _in_projection(x2d, mod, ln_g, ln_b, w_in_bf, *, groups, tm, names):
    n, d = x2d.shape
    width = w_in_bf.shape[1] // 5
    n_heads = width // HEAD_LANES
    per_group = n // groups // tm
    mod_groups, r = mod.shape[0], mod.shape[1]

    def mod_spec(comp):
        return pl.BlockSpec((None, r, d),
                            lambda i: ((i // per_group) % mod_groups, 0, comp))

    row_spec = lambda rows, w: pl.BlockSpec((rows, w), lambda i: (i, 0))
    t_spec = pl.BlockSpec((None, width, tm), lambda i: (i // per_group, 0, i % per_group))
    t_shape = (groups, width, n // groups)
    dtypes = {"xp": F32, "q": BF16, "k": F32, "v": F32, "u": F32, "k_b": BF16, "v_heads": F32,
              "kt": F32, "qt_b": BF16, "vt_b": BF16}
    out_specs, out_shape = [], []
    for nm in names:
        if nm in ("kt", "qt_b", "vt_b"):
            spec, shape = t_spec, t_shape
        elif nm == "v_heads":
            spec, shape = row_spec(tm * n_heads, HEAD_LANES), (n * n_heads, HEAD_LANES)
        elif nm == "xp":
            spec, shape = row_spec(tm, d), (n, d)
        else:
            spec, shape = row_spec(tm, width), (n, width)
        out_specs.append(spec)
        out_shape.append(jax.ShapeDtypeStruct(shape, dtypes[nm]))
    outs = pl.pallas_call(
        functools.partial(_inproj_kernel, width=width, names=tuple(names)),
        grid=(n // tm,),
        in_specs=[row_spec(tm, d), mod_spec(0), mod_spec(1),
                  _const_spec((1, d)), _const_spec((1, d)),
                  _const_spec(w_in_bf.shape)],
        out_specs=out_specs,
        out_shape=out_shape,
        compiler_params=pltpu.CompilerParams(
            dimension_semantics=("arbitrary",), vmem_limit_bytes=VMEM_LIMIT),
        name="in_projection",
    )(x2d, mod, mod, ln_g.reshape(1, d), ln_b.reshape(1, d), w_in_bf)
    return dict(zip(names, outs))


def _lambda_value(lq1_ref, lk1_ref, lq2_ref, lk2_ref, lambda_init):
    s1 = jnp.sum(lq1_ref[...] * lk1_ref[...], axis=-1, keepdims=True)
    s2 = jnp.sum(lq2_ref[...] * lk2_ref[...], axis=-1, keepdims=True)
    return jnp.exp(s1) - jnp.exp(s2) + lambda_init


def _sub_ln(o, g, lambda_init):
    ms = jnp.mean(o * o, axis=-1, keepdims=True)
    return o * lax.rsqrt(ms + LN_EPS) * g * (1.0 - lambda_init)


def _prompt_attn_kernel(qi_tab, ki_tab, qt_ref, k_ref, vt_ref, lq1_ref, lk1_ref, lq2_ref, lk2_ref,
                        g_ref, o_ref, qaug_scr, pos_scr, s_scr, m_scr, acc_scr, *,
                        n_heads, tq, tk, lambda_init):
    pair = pl.program_id(1)
    qi = qi_tab[pair]
    ki = ki_tab[pair]

    @pl.when(ki == 0)
    def _init():
        m_scr[...] = jnp.full(m_scr.shape, NEG_BIG, F32)
        acc_scr[...] = jnp.zeros(acc_scr.shape, F32)
        r = lax.broadcasted_iota(jnp.int32, (HEAD_LANES, tq), 0)
        for h in range(n_heads):
            qt = qt_ref[h * HEAD_LANES:(h + 1) * HEAD_LANES, :].astype(F32)
            slope = _alibi_slope(h, n_heads)
            bot = jnp.where(r == 0, ALIBI_POS_SPLIT * slope, jnp.where(r == 1, slope, 0.0))
            for j in range(2):
                own = (r < HEAD_DIM) if j == 0 else (r >= HEAD_DIM)
                top = jnp.where(own, qt, 0.0)
                qaug_scr[2 * h + j] = jnp.concatenate([top, bot], axis=0).astype(BF16)

    def step(on_diagonal):
        kr = lax.broadcasted_iota(jnp.int32, (tk, LANE), 0)
        lane = lax.broadcasted_iota(jnp.int32, (tk, LANE), 1)
        kpos = ki * tk + kr
        pos_hi = kpos // ALIBI_POS_SPLIT
        pos_lo = kpos % ALIBI_POS_SPLIT
        pos_scr[...] = jnp.where(lane == 0, pos_hi, jnp.where(lane == 1, pos_lo, 0)
                                 ).astype(F32).astype(BF16)
        ones_row = jnp.where(lax.broadcasted_iota(jnp.int32, (2 * SUBLANE, tk), 0) == 0,
                             1.0, 0.0).astype(BF16)
        if on_diagonal:
            key = lax.broadcasted_iota(jnp.int32, (tk, tq), 0)
            qry = lax.broadcasted_iota(jnp.int32, (tk, tq), 1)
            visible = key <= qry

        def head_cols(slab):
            return slice((slab // 2) * HEAD_LANES, (slab // 2 + 1) * HEAD_LANES)

        def scores(slab):
            kaug = jnp.concatenate([k_ref[:, head_cols(slab)], pos_scr[...]], axis=1)
            s_scr[slab % 2] = _dot(kaug, qaug_scr[slab])

        def softmax_update(slab):
            s = s_scr[slab % 2]
            if on_diagonal:
                s = jnp.where(visible, s, NEG_BIG)
            m_prev = m_scr[slab]
            m_new = jnp.maximum(m_prev, jnp.max(s, axis=0, keepdims=True))
            alpha = jnp.exp(m_prev - m_new)
            p = jnp.exp(s - m_new).astype(BF16)
            vaug = jnp.concatenate([vt_ref[head_cols(slab), :], ones_row], axis=0)
            acc_scr[slab] = alpha * acc_scr[slab] + _dot(vaug, p)
            m_scr[slab] = m_new

        n_slabs = 2 * n_heads
        scores(0)
        for slab in range(n_slabs):
            if slab + 1 < n_slabs:
                scores(slab + 1)
            softmax_update(slab)

    @pl.when(ki < qi)
    def _below_diagonal():
        step(False)

    @pl.when(ki == qi)
    def _diagonal_and_finalize():
        step(True)
        lam = _lambda_value(lq1_ref, lk1_ref, lq2_ref, lk2_ref, lambda_init)
        def normalised(slab):
            acc = acc_scr[slab]
            return acc[:HEAD_LANES] * (1.0 / acc[HEAD_LANES:HEAD_LANES + 1])

        for h in range(n_heads):
            o = normalised(2 * h) - lam * normalised(2 * h + 1)
            ms = jnp.mean(o * o, axis=0, keepdims=True)
            o = o * lax.rsqrt(ms + LN_EPS) * g_ref[...] * (1.0 - lambda_init)
            o_ref[:, h * HEAD_LANES:(h + 1) * HEAD_LANES] = o.T.astype(o_ref.dtype)


def _prompt_attention(qt_b, k_b, vt_b, lam_params, subln_g, lambda_init, tq=512):
    bsz, width, seq = qt_b.shape
    n_heads = width // HEAD_LANES
    tk = tq
    nq = seq // tq
    assert (seq - 1) // ALIBI_POS_SPLIT < 256, "key position parts must be exact in bf16"
    lq1, lk1, lq2, lk2 = [p.reshape(1, HEAD_DIM) for p in lam_params]
    pairs = [(i, j) for i in range(nq) for j in range(i + 1)]
    qi_tab = jnp.asarray([p[0] for p in pairs], jnp.int32)
    ki_tab = jnp.asarray([p[1] for p in pairs], jnp.int32)
    qt_spec = pl.BlockSpec((None, width, tq), lambda b, s, qi, ki: (b, 0, qi[s]))
    k_spec = pl.BlockSpec((tk, width), lambda b, s, qi, ki: (b * nq + ki[s], 0))
    vt_spec = pl.BlockSpec((None, width, tk), lambda b, s, qi, ki: (b, 0, ki[s]))
    o_spec = pl.BlockSpec((tq, width), lambda b, s, qi, ki: (b * nq + qi[s], 0))
    small = lambda r, w: pl.BlockSpec((r, w), lambda b, s, qi, ki: (0, 0))
    grid_spec = pltpu.PrefetchScalarGridSpec(
        num_scalar_prefetch=2,
        grid=(bsz, len(pairs)),
        in_specs=[qt_spec, k_spec, vt_spec,
                  small(1, HEAD_DIM), small(1, HEAD_DIM), small(1, HEAD_DIM), small(1, HEAD_DIM),
                  small(HEAD_LANES, 1)],
        out_specs=o_spec,
        scratch_shapes=[pltpu.VMEM((2 * n_heads, 2 * HEAD_LANES, tq), BF16),
                        pltpu.VMEM((tk, LANE), BF16),
                        pltpu.VMEM((2, tk, tq), F32),
                        pltpu.VMEM((2 * n_heads, 1, tq), F32),
                        pltpu.VMEM((2 * n_heads, HEAD_LANES + 2 * SUBLANE, tq), F32)],
    )
    return pl.pallas_call(
        functools.partial(_prompt_attn_kernel, n_heads=n_heads, tq=tq, tk=tk,
                          lambda_init=lambda_init),
        grid_spec=grid_spec,
        out_shape=jax.ShapeDtypeStruct((bsz * seq, width), BF16),
        compiler_params=pltpu.CompilerParams(
            dimension_semantics=("arbitrary", "arbitrary"), vmem_limit_bytes=VMEM_LIMIT),
        name="prompt_attention",
    )(qi_tab, ki_tab, qt_b, k_b, vt_b, lq1, lk1, lq2, lk2, subln_g.reshape(HEAD_LANES, 1))


def _sample_attn_tile(q_ref, kn_ref, vn_ref, lq1_ref, lk1_ref, lq2_ref, lk2_ref, g_ref,
                      kt_refs, v_refs, o_ref, s_scr, *, page, n_new, n_heads, lambda_init):
    n_pages = len(kt_refs)
    width = n_heads * HEAD_LANES
    n_maps = 2 * n_heads
    rows = n_new * n_maps
    past = n_pages * page

    q = q_ref[0].astype(F32)
    mp = lax.broadcasted_iota(jnp.int32, (n_maps, width), 0)
    cl = lax.broadcasted_iota(jnp.int32, (n_maps, width), 1)
    map_lanes = (cl // HEAD_DIM) == mp
    qbd32 = jnp.concatenate(
        [jnp.where(map_lanes, jnp.broadcast_to(q[i:i + 1, :], (n_maps, width)), 0.0)
         for i in range(n_new)], axis=0)
    qbd = qbd32.astype(BF16)

    for p in range(n_pages):
        s_scr[:, p * page:(p + 1) * page] = _dot(qbd, kt_refs[p][...].astype(BF16))

    rmap = lax.broadcasted_iota(jnp.int32, (rows, 1), 0) % n_maps
    rqry = lax.broadcasted_iota(jnp.int32, (rows, 1), 0) // n_maps
    slope = jnp.zeros((rows, 1), F32)
    for h in range(n_heads):
        slope = jnp.where(rmap // 2 == h, _alibi_slope(h, n_heads), slope)
    tpos = lax.broadcasted_iota(jnp.int32, (1, past), 1).astype(F32)
    s = s_scr[...] + slope * tpos

    kn = kn_ref[0]
    vn = vn_ref[0]
    s_new = []
    for j in range(n_new):
        sj = jnp.sum(qbd32 * kn[j:j + 1, :], axis=-1, keepdims=True) + slope * float(past + j)
        s_new.append(jnp.where(rqry >= j, sj, NEG_BIG))
    m = jnp.max(s, axis=-1, keepdims=True)
    for sj in s_new:
        m = jnp.maximum(m, sj)
    pr = jnp.exp(s - m)
    l = jnp.sum(pr, axis=-1, keepdims=True)
    prb = pr.astype(BF16)
    p_new = []
    for sj in s_new:
        pj = jnp.exp(sj - m)
        l = l + pj
        p_new.append(pj)
    inv_l = 1.0 / l

    lam = _lambda_value(lq1_ref, lk1_ref, lq2_ref, lk2_ref, lambda_init)
    mrow = lax.broadcasted_iota(jnp.int32, (n_maps, 1), 0)
    for h in range(n_heads):
        cols = slice(h * HEAD_LANES, (h + 1) * HEAD_LANES)
        out = jnp.zeros((rows, HEAD_LANES), F32)
        for j, pj in enumerate(p_new):
            out = out + pj * vn[j:j + 1, cols]
        for p in range(n_pages):
            vh = v_refs[p][pl.ds(h, page, stride=n_heads), :].astype(BF16)
            out = out + _dot(prb[:, p * page:(p + 1) * page], vh)
        out = out * inv_l
        weight = jnp.where(mrow == 2 * h, 1.0, jnp.where(mrow == 2 * h + 1, -lam, 0.0))
        for i in range(n_new):
            o = jnp.sum(out[i * n_maps:(i + 1) * n_maps, :] * weight, axis=0, keepdims=True)
            o_ref[0, i:i + 1, cols] = _sub_ln(o, g_ref[...], lambda_init)
    o_ref[0, n_new:, :] = jnp.zeros((SUBLANE - n_new, width), F32)


def _sample_attn_prompt_conv_kernel(pt_ref, q_ref, kn_ref, vn_ref, lq1_ref, lk1_ref, lq2_ref,
                                    lk2_ref, g_ref, u_ref, cw_ref, cb_ref, clg_ref, clb_ref,
                                    kt_hbm, v_hbm, o_ref, conv_o_ref,
                                    s_scr, buf, wrep, cv_scr, kbuf, vbuf, sem, *,
                                    n_seq, n_pages, page, n_new, n_heads, lambda_init,
                                    tiles_per_seq, tt, taps, halo):
    step = pl.program_id(0)

    def page_copies(seq, slot, pages):
        copies = []
        for p in pages:
            page_id = pt_ref[seq * n_pages + p]
            copies.append(pltpu.make_async_copy(kt_hbm.at[page_id], kbuf.at[slot, p],
                                                sem.at[slot, 0, p]))
            copies.append(pltpu.make_async_copy(v_hbm.at[page_id], vbuf.at[slot, p],
                                                sem.at[slot, 1, p]))
        return copies

    all_pages = range(n_pages)

    @pl.when(step == 0)
    def _prime():
        for ahead in range(PAGES_AHEAD):
            for copy in page_copies(ahead, ahead, all_pages):
                copy.start()

    for slot in range(PAGE_SLOTS):
        seq = step * PAGE_SLOTS + slot
        next_slot = (slot + PAGES_AHEAD) % PAGE_SLOTS
        next_seq = seq + PAGES_AHEAD
        next_seq = jnp.where(next_seq >= n_seq, next_seq - n_seq, next_seq)

        def fetch_next(i, n_blocks, next_seq=next_seq, next_slot=next_slot):
            per_block = -(-n_pages // n_blocks)
            pages = range(i * per_block, min((i + 1) * per_block, n_pages))
            for copy in page_copies(next_seq, next_slot, pages):
                copy.start()

        rows = pl.ds(slot * tt, tt)
        _conv_tile(seq % tiles_per_seq == 0, u_ref.at[rows], cw_ref, cb_ref, clg_ref, clb_ref,
                   conv_o_ref.at[rows], buf, wrep, cv_scr, tt=tt, taps=taps, halo=halo, rc=tt,
                   before_block=fetch_next)
        for copy in page_copies(seq, slot, all_pages):
            copy.wait()
        one = pl.ds(slot, 1)
        _sample_attn_tile(q_ref.at[one], kn_ref.at[one], vn_ref.at[one], lq1_ref, lk1_ref,
                          lq2_ref, lk2_ref, g_ref,
                          [kbuf.at[slot, p] for p in all_pages],
                          [vbuf.at[slot, p] for p in all_pages],
                          o_ref.at[one], s_scr, page=page, n_new=n_new, n_heads=n_heads,
                          lambda_init=lambda_init)

    @pl.when(step == pl.num_programs(0) - 1)
    def _drain():
        for ahead in range(PAGES_AHEAD):
            for copy in page_copies(ahead, ahead, all_pages):
                copy.wait()


def _sample_attention_and_prompt_conv(q8, kn8, vn8, cache_kt, cache_v, page_table, lam_params,
                                      subln_g, n_new, lambda_init,
                                      u, conv_w, conv_b, conv_ln_g, conv_ln_b, seq):
    n_seq, _, width = q8.shape
    n_pages = page_table.shape[1]
    page = cache_kt.shape[2]
    n_heads = width // HEAD_LANES
    assert 2 * n_heads == SUBLANE and n_new <= SUBLANE
    assert n_seq % PAGE_SLOTS == 0 and 0 < PAGES_AHEAD < PAGE_SLOTS
    n_rows, conv_width = u.shape
    taps = conv_w.shape[0]
    halo = -(-(taps - 1) // SUBLANE) * SUBLANE
    tt = n_rows // n_seq
    assert tt * n_seq == n_rows and seq % tt == 0 and tt >= halo and tt % SUBLANE == 0
    n_slab = conv_width // LANE
    lq1, lk1, lq2, lk2 = [p.reshape(1, HEAD_DIM) for p in lam_params]
    seq_spec = pl.BlockSpec((PAGE_SLOTS, SUBLANE, width), lambda s, pt: (s, 0, 0))
    small = lambda r, w: pl.BlockSpec((r, w), lambda s, pt: (0, 0))
    conv_rows = pl.BlockSpec((PAGE_SLOTS * tt, conv_width), lambda s, pt: (s, 0))
    hbm = pl.BlockSpec(memory_space=pl.ANY)

    grid_spec = pltpu.PrefetchScalarGridSpec(
        num_scalar_prefetch=1,
        grid=(n_seq // PAGE_SLOTS,),
        in_specs=[seq_spec, seq_spec, seq_spec,
                  small(1, HEAD_DIM), small(1, HEAD_DIM), small(1, HEAD_DIM), small(1, HEAD_DIM),
                  small(1, HEAD_LANES),
                  conv_rows, small(taps, conv_width), small(1, conv_width),
                  small(1, conv_width), small(1, conv_width), hbm, hbm],
        out_specs=[seq_spec, conv_rows],
        scratch_shapes=[pltpu.VMEM((n_new * 2 * n_heads, n_pages * page), F32),
                        pltpu.VMEM((n_slab, halo + tt, LANE), F32),
                        pltpu.VMEM((taps + 1, n_slab, SUBLANE, LANE), F32),
                        pltpu.VMEM((n_slab, tt, LANE), F32),
                        pltpu.VMEM((PAGE_SLOTS, n_pages) + cache_kt.shape[1:], F32),
                        pltpu.VMEM((PAGE_SLOTS, n_pages) + cache_v.shape[1:], F32),
                        pltpu.SemaphoreType.DMA((PAGE_SLOTS, 2, n_pages))],
    )
    vec = lambda a: a.reshape(1, -1)
    return pl.pallas_call(
        functools.partial(_sample_attn_prompt_conv_kernel, n_seq=n_seq, n_pages=n_pages,
                          page=page, n_new=n_new, n_heads=n_heads, lambda_init=lambda_init,
                          tiles_per_seq=seq // tt, tt=tt, taps=taps, halo=halo),
        grid_spec=grid_spec,
        out_shape=[jax.ShapeDtypeStruct((n_seq, SUBLANE, width), F32),
                   jax.ShapeDtypeStruct((n_rows, conv_width), BF16)],
        compiler_params=pltpu.CompilerParams(
            dimension_semantics=("arbitrary",), vmem_limit_bytes=VMEM_LIMIT),
        name="sample_attention_prompt_conv",
    )(page_table.reshape(-1), q8, kn8, vn8, lq1, lk1, lq2, lk2, vec(subln_g),
      u, conv_w, vec(conv_b), vec(conv_ln_g), vec(conv_ln_b), cache_kt, cache_v)


def _conv_tile(first, u_ref, w_ref, b_ref, lg_ref, lb_ref, o_ref, buf, wrep, cv_scr, *,
               tt, taps, halo, rc, before_block):
    n_slab = buf.shape[0]

    @pl.when(first)
    def _start_of_sequence():
        buf[:, 0:halo, :] = jnp.zeros((n_slab, halo, LANE), F32)
        for s in range(n_slab):
            cols = slice(s * LANE, (s + 1) * LANE)
            wrep[taps, s] = jnp.broadcast_to(b_ref[:, cols], (SUBLANE, LANE))
            for j in range(taps):
                wrep[j, s] = jnp.broadcast_to(w_ref[j:j + 1, cols], (SUBLANE, LANE))

    for s in range(n_slab):
        buf[s, halo:halo + tt, :] = u_ref[:, s * LANE:(s + 1) * LANE]
    base = halo - (taps - 1)
    groups = rc // SUBLANE

    def block_conv(i):
        s = i % n_slab
        r0 = (i // n_slab) * rc
        acc = jnp.broadcast_to(wrep[taps, s][None], (groups, SUBLANE, LANE))
        for phase in range(SUBLANE):
            phase_taps = range(phase, taps, SUBLANE)
            n_groups = groups + len(phase_taps) - 1
            x = buf[s, pl.ds(base + r0 + phase, n_groups * SUBLANE), :].reshape(
                n_groups, SUBLANE, LANE)
            for a, j in enumerate(phase_taps):
                acc = acc + wrep[j, s][None] * x[a:a + groups]
        cv_scr[s, pl.ds(r0, rc), :] = acc.reshape(rc, LANE)

    n_blocks = (tt // rc) * n_slab
    for i in range(n_blocks):
        before_block(i, n_blocks)
        block_conv(i)
    cv = jnp.concatenate([cv_scr[s] for s in range(n_slab)], axis=1)
    o_ref[...] = _silu(_ln(cv, lg_ref[...], lb_ref[...])).astype(o_ref.dtype)
    for s in range(n_slab):
        buf[s, 0:halo, :] = buf[s, tt:tt + halo, :]


def _sample_conv_kernel(full_ref, w_ref, b_ref, lg_ref, lb_ref, o_ref, *, n_new, n_seq, taps):
    for t in range(n_new):
        cv = jnp.broadcast_to(b_ref[...], (n_seq, full_ref.shape[2]))
        for j in range(taps):
            cv = cv + w_ref[j:j + 1, :] * full_ref[t + j]
        o_ref[t] = _silu(_ln(cv, lg_ref[...], lb_ref[...])).astype(o_ref.dtype)


def _sample_conv(full_tm, conv_w, conv_b, ln_g, ln_b, n_new):
    n_time, n_seq, width = full_tm.shape
    taps = conv_w.shape[0]
    return pl.pallas_call(
        functools.partial(_sample_conv_kernel, n_new=n_new, n_seq=n_seq, taps=taps),
        out_shape=jax.ShapeDtypeStruct((n_new, n_seq, width), BF16),
        compiler_params=pltpu.CompilerParams(vmem_limit_bytes=VMEM_LIMIT),
        name="sample_conv",
    )(full_tm, conv_w, conv_b.reshape(1, width), ln_g.reshape(1, width), ln_b.reshape(1, width))


def _ffn_kernel(*refs, has_state, tm, pad, tshift, fc, d_ff, alpha):
    (xp_ref, attn_ref, conv_ref, g1_ref, sh2_ref, sc2_ref, g2_ref,
     wout_ref, l1g_ref, l1b_ref, wup_ref, cw_ref, cb_ref, wdn_ref, l2g_ref, l2b_ref) = refs[:16]
    refs = refs[16:]
    if has_state:
        state_ref, refs = refs[0], refs[1:]
    y_ref, tail_ref, ubuf, acc_ref = refs
    half = attn_ref.shape[1]
    n_slab = fc // LANE

    if has_state:
        hist_ref = state_ref
    else:
        hist_ref = tail_ref

        @pl.when(pl.program_id(1) == 0)
        def _zero_history():
            tail_ref[...] = jnp.zeros(tail_ref.shape, F32)

    def mod_rows(ref):
        m = ref[...]
        reps = 1 if m.shape[0] == 1 else tm // m.shape[0]
        return m if reps == 1 else jnp.concatenate([m] * reps, axis=0)

    mix = _dot(attn_ref[...], wout_ref[0:half, :]) + _dot(conv_ref[...], wout_ref[half:, :])
    x1 = _ln(alpha * xp_ref[...] + mod_rows(g1_ref) * mix, l1g_ref[...], l1b_ref[...])
    h2 = (x1 * (1.0 + mod_rows(sc2_ref)) + mod_rows(sh2_ref)).astype(BF16)

    n_chunks = d_ff // fc

    def up_project(c):
        for part in range(2):
            col = part * d_ff + c * fc
            up = _dot(h2, wup_ref[:, col:col + fc])
            for s in range(n_slab):
                cols = slice(col + s * LANE, col + (s + 1) * LANE)
                slab = ((c % 2) * 2 + part) * n_slab + s
                ubuf[slab, 0:pad, :] = hist_ref[:, cols]
                ubuf[slab, pad:pad + tm, :] = up[:, s * LANE:(s + 1) * LANE]
            tail_ref[:, col:col + fc] = up[tm - pad:, :]

    def conv_half(c, part):
        parts = []
        for s in range(n_slab):
            col = part * d_ff + c * fc + s * LANE
            cols = slice(col, col + LANE)
            slab = ((c % 2) * 2 + part) * n_slab + s
            parts.append(cw_ref[0:1, cols] * ubuf[slab, pad - 2 * tshift:pad - 2 * tshift + tm, :]
                         + cw_ref[1:2, cols] * ubuf[slab, pad - tshift:pad - tshift + tm, :]
                         + cw_ref[2:3, cols] * ubuf[slab, pad:pad + tm, :] + cb_ref[:, cols])
        return jnp.concatenate(parts, axis=1)

    up_project(0)
    for c in range(n_chunks):
        if c + 1 < n_chunks:
            up_project(c + 1)
        g = (_silu(conv_half(c, 0)) * conv_half(c, 1)).astype(BF16)
        contrib = _dot(g, wdn_ref[c * fc:(c + 1) * fc, :])
        if c == 0:
            acc_ref[...] = contrib
        else:
            acc_ref[...] += contrib

    y_ref[...] = _ln(alpha * x1 + mod_rows(g2_ref) * acc_ref[...], l2g_ref[...], l2b_ref[...])


def _ffn(xp2d, attn, conv, mod, w_out_bf, ln1, w_up_bf, ffn_conv_w, ffn_conv_b,
         w_down_bf, ln2, *, groups, tiles, tm, pad, tshift, state, alpha, fc=256):
    n, d = xp2d.shape
    half = attn.shape[1]
    d_ff = w_down_bf.shape[0]
    r = mod.shape[1]
    has_state = state is not None
    assert pad == 2 * tshift or not has_state
    assert tm >= pad
    row = lambda w: pl.BlockSpec((tm, w), lambda b, t: (b * tiles + t, 0))
    mod_spec = lambda comp: pl.BlockSpec((None, r, d), lambda b, t: (b, 0, comp))
    vec = lambda a: a.reshape(1, -1)
    args = [xp2d, attn, conv, mod, mod, mod, mod,
            w_out_bf, vec(ln1[0]), vec(ln1[1]), w_up_bf, ffn_conv_w, vec(ffn_conv_b),
            w_down_bf, vec(ln2[0]), vec(ln2[1])]
    in_specs = [row(d), row(half), row(half), mod_spec(2), mod_spec(3), mod_spec(4), mod_spec(5),
                _const_spec(w_out_bf.shape), _const_spec((1, d)), _const_spec((1, d)),
                _const_spec(w_up_bf.shape), _const_spec(ffn_conv_w.shape),
                _const_spec((1, 2 * d_ff)), _const_spec(w_down_bf.shape),
                _const_spec((1, d)), _const_spec((1, d))]
    if has_state:
        args.append(state)
        in_specs.append(_const_spec(state.shape))
    y, tail = pl.pallas_call(
        functools.partial(_ffn_kernel, has_state=has_state, tm=tm, pad=pad,
                          tshift=tshift, fc=fc, d_ff=d_ff, alpha=alpha),
        grid=(groups, tiles),
        in_specs=in_specs,
        out_specs=[row(d), pl.BlockSpec((None, pad, 2 * d_ff), lambda b, t: (b, 0, 0))],
        out_shape=[jax.ShapeDtypeStruct((n, d), F32),
                   jax.ShapeDtypeStruct((groups, pad, 2 * d_ff), F32)],
        scratch_shapes=[pltpu.VMEM((4 * (fc // LANE), pad + tm, LANE), F32),
                        pltpu.VMEM((tm, d), F32)],
        compiler_params=pltpu.CompilerParams(
            dimension_semantics=("arbitrary", "arbitrary"), vmem_limit_bytes=VMEM_LIMIT),
        name="ffn",
    )(*args)
    return y, tail


def kernel(x_prompt, x_sample, c_prompt, c_sample, cache_k, cache_v, page_table, state_conv, state_ffn, ln_emb_g, ln_emb_b, w_ada, b_ada, w_in, lambda_q1, lambda_k1, lambda_q2, lambda_k2, subln_g, conv_w, conv_b, conv_ln_g, conv_ln_b, w_out, ln1_g, ln1_b, w_up, ffn_conv_w, ffn_conv_b, w_down, ln2_g, ln2_b):
    bsz, seq, d = x_prompt.shape
    n_seq, n_new, _ = x_sample.shape
    depth = w_ada.shape[0]
    assert depth == 1, "the prompt/sample activations are threaded for a single layer"
    n_phys, page = cache_k.shape[1], cache_k.shape[2]
    width = cache_k.shape[3] * cache_k.shape[4]
    d_ff = w_down.shape[1]
    conv_taps = conv_w.shape[1]
    ffn_taps = ffn_conv_w.shape[1]
    assert ffn_taps == 3
    alpha = (2 * depth) ** 0.25
    tm = 512
    n_rows_s = n_seq * n_new
    n_kh = width // HEAD_DIM
    n_vh = width // HEAD_LANES

    l = 0
    lambda_init = 0.8 - 0.6 * math.exp(-0.3 * l)
    lam_params = (lambda_q1[l], lambda_k1[l], lambda_q2[l], lambda_k2[l])
    w_in_bf = w_in[l].astype(BF16)
    w_out_bf = w_out[l].astype(BF16)
    w_up_bf = w_up[l].astype(BF16)
    w_down_bf = w_down[l].astype(BF16)

    c_rows = bsz + n_seq
    c_all = jnp.concatenate([c_prompt, c_sample], axis=0)
    c_all = jnp.pad(c_all, ((0, -c_rows % (2 * SUBLANE)), (0, 0)))
    mod = _modulation(c_all, w_ada[l], b_ada[l])
    mod_p = mod[:bsz].reshape(bsz, 1, 6 * d)
    mod_s = mod[bsz:c_rows]

    pp = _in_projection(x_prompt.reshape(bsz * seq, d), mod_p, ln_emb_g, ln_emb_b, w_in_bf,
                        groups=bsz, tm=tm,
                        names=("xp", "qt_b", "k_b", "kt", "v_heads", "vt_b", "u"))
    u = pp["u"]

    xs_tm = x_sample.swapaxes(0, 1).reshape(n_rows_s, d)
    ps = _in_projection(xs_tm, mod_s[None], ln_emb_g, ln_emb_b, w_in_bf, groups=n_new, tm=n_seq,
                        names=("xp", "q", "k", "kt", "v", "u"))
    qs, ks, kts, vs, us = ps["q"], ps["k"], ps["kt"], ps["v"], ps["u"]
    to_sm = lambda a: a.reshape(n_new, n_seq, -1).swapaxes(0, 1)
    pad8 = lambda a: jnp.pad(to_sm(a), ((0, 0), (0, SUBLANE - n_new), (0, 0)))
    cache_kt = cache_k[l].transpose(0, 2, 3, 1).reshape(n_phys, width, page)
    cache_vr = cache_v[l].reshape(n_phys, page * n_vh, HEAD_LANES)
    attn_s8, conv = _sample_attention_and_prompt_conv(
        pad8(qs), pad8(ks), pad8(vs), cache_kt, cache_vr, page_table, lam_params, subln_g[l],
        n_new, lambda_init, u, conv_w[l], conv_b[l], conv_ln_g[l], conv_ln_b[l], seq)
    attn_s = attn_s8[:, :n_new, :].astype(BF16).swapaxes(0, 1).reshape(n_rows_s, width)

    attn = _prompt_attention(pp["qt_b"], pp["k_b"], pp["vt_b"], lam_params, subln_g[l],
                             lambda_init)
    y_p, tail_p = _ffn(pp["xp"], attn, conv, mod_p, w_out_bf,
                       (ln1_g[l], ln1_b[l]), w_up_bf, ffn_conv_w[l], ffn_conv_b[l], w_down_bf,
                       (ln2_g[l], ln2_b[l]), groups=bsz, tiles=seq // tm, tm=tm,
                       pad=SUBLANE, tshift=1, state=None, alpha=alpha)
    y_prompt = y_p.reshape(bsz, seq, d)
    k_prompt = pp["kt"].reshape(bsz, n_kh, HEAD_DIM, seq).transpose(0, 3, 1, 2)[None]
    v_prompt = pp["v_heads"].reshape(1, bsz, seq, n_vh, HEAD_LANES)
    conv_prompt = u.reshape(bsz, seq, -1)[None, :, seq - (conv_taps - 1):, :]
    ffn_prompt = tail_p[None, :, SUBLANE - (ffn_taps - 1):, :]

    full_tm = jnp.concatenate([state_conv[l].swapaxes(0, 1), us.reshape(n_new, n_seq, -1)], axis=0)
    conv_s = _sample_conv(full_tm, conv_w[l], conv_b[l], conv_ln_g[l], conv_ln_b[l],
                          n_new).reshape(n_rows_s, -1)
    state_tm = state_ffn[l].swapaxes(0, 1).reshape((ffn_taps - 1) * n_seq, 2 * d_ff)
    y_s, tail_s = _ffn(ps["xp"], attn_s, conv_s, mod_s[None],
                       w_out_bf, (ln1_g[l], ln1_b[l]), w_up_bf,
                       ffn_conv_w[l], ffn_conv_b[l], w_down_bf, (ln2_g[l], ln2_b[l]),
                       groups=1, tiles=1, tm=n_rows_s, pad=(ffn_taps - 1) * n_seq,
                       tshift=n_seq, state=state_tm, alpha=alpha)
    y_sample = y_s.reshape(n_new, n_seq, d).swapaxes(0, 1)
    k_sample = kts.reshape(n_new, n_kh, HEAD_DIM, n_seq).transpose(3, 0, 1, 2)[None]
    v_sample = to_sm(vs).reshape(1, n_seq, n_new, n_vh, HEAD_LANES)
    conv_sample = full_tm[n_new:].swapaxes(0, 1)[None]
    ffn_sample = tail_s.reshape(ffn_taps - 1, n_seq, 2 * d_ff).swapaxes(0, 1)[None]

    return (y_prompt, y_sample, k_prompt, v_prompt, conv_prompt, ffn_prompt,
            k_sample, v_sample, conv_sample, ffn_sample)
```

```python
import functools
import math

import jax
import jax.numpy as jnp
from jax import lax
from jax.experimental import pallas as pl
from jax.experimental.pallas import tpu as pltpu

F32 = jnp.float32
BF16 = jnp.bfloat16

LN_EPS = 1e-5
HEAD_DIM = 64
HEAD_LANES = 2 * HEAD_DIM
LANE = 128
SUBLANE = 8
VMEM_LIMIT = 56 * 1024 * 1024
NEG_BIG = -1e30
ALIBI_POS_SPLIT = 64
PAGE_SLOTS = 4
PAGES_AHEAD = 2
MXU_TILE = 256
ROW_TILE = 2 * MXU_TILE

def _ln(x, g, b):
    mu = jnp.mean(x, axis=-1, keepdims=True)
    xc = x - mu
    var = jnp.mean(xc * xc, axis=-1, keepdims=True)
    return xc * lax.rsqrt(var + LN_EPS) * g + b


def _silu(x):
    return x * jax.nn.sigmoid(x)


def _dot(a, b):
    return jnp.dot(a, b, preferred_element_type=F32)


def _const_spec(shape):
    nd = len(shape)
    return pl.BlockSpec(shape, lambda *_: (0,) * nd, pipeline_mode=pl.Buffered(1))


def _alibi_slope(h, n_heads):
    return 2.0 ** (-8.0 * (h + 1) / n_heads)


def _split_bf16(x):
    hi = x.astype(BF16)
    return hi, (x - hi.astype(F32)).astype(BF16)


def _mod_kernel(c_ref, w_ref, b_ref, o_ref):
    a_hi, a_lo = _split_bf16(_silu(c_ref[...]))
    w_hi, w_lo = _split_bf16(w_ref[...])
    o_ref[...] = _dot(a_hi, w_hi) + (_dot(a_hi, w_lo) + _dot(a_lo, w_hi)) + b_ref[...]


def _modulation(c_all, w_ada, b_ada):
    rows, d = c_all.shape
    n_out = w_ada.shape[1]
    return pl.pallas_call(
        _mod_kernel,
        grid=(n_out // d,),
        in_specs=[pl.BlockSpec((rows, d), lambda j: (0, 0)),
                  pl.BlockSpec((d, d), lambda j: (0, j)),
                  pl.BlockSpec((1, d), lambda j: (0, j))],
        out_specs=pl.BlockSpec((rows, d), lambda j: (0, j)),
        out_shape=jax.ShapeDtypeStruct((rows, n_out), F32),
        compiler_params=pltpu.CompilerParams(vmem_limit_bytes=VMEM_LIMIT),
        name="modulation",
    )(c_all, w_ada, b_ada.reshape(1, n_out))


def _inproj_kernel(x_ref, sh_ref, sc_ref, lg_ref, lb_ref, w_ref, *out_refs, width, names):
    out = dict(zip(names, out_refs))
    tm = x_ref.shape[0]
    xp = _ln(x_ref[...], lg_ref[...], lb_ref[...])
    h = (xp * (1.0 + sc_ref[...]) + sh_ref[...]).astype(BF16)

    def proj(i):
        return _dot(h, w_ref[:, i * width:(i + 1) * width])

    def emit(name, value):
        if name in out:
            out[name][...] = value.astype(out[name].dtype)

    emit("xp", xp)
    q = proj(0) * (HEAD_DIM ** -0.5)
    emit("q", q)
    if "qt_b" in out:
        emit("qt_b", q.T)
    k = proj(1)
    emit("k", k)
    emit("k_b", k)
    emit("kt", k.T)
    v = proj(2)
    emit("v", v)
    if "vt_b" in out:
        emit("vt_b", v.T)
    if "v_heads" in out:
        n_heads = width // HEAD_LANES
        for hd in range(n_heads):
            out["v_heads"][pl.ds(hd, tm, stride=n_heads), :] = (
                v[:, hd * HEAD_LANES:(hd + 1) * HEAD_LANES])
    emit("u", proj(3) * jax.nn.sigmoid(proj(4)))


def _in_projection(x2d, mod, ln_g, ln_b, w_in_bf, *, groups, tm, names):
    n, d = x2d.shape
    width = w_in_bf.shape[1] // 5
    n_heads = width // HEAD_LANES
    per_group = n // groups // tm
    mod_groups, r = mod.shape[0], mod.shape[1]

    def mod_spec(comp):
        return pl.BlockSpec((None, r, d),
                            lambda i: ((i // per_group) % mod_groups, 0, comp))

    row_spec = lambda rows, w: pl.BlockSpec((rows, w), lambda i: (i, 0))
    t_spec = pl.BlockSpec((None, width, tm), lambda i: (i // per_group, 0, i % per_group))
    t_shape = (groups, width, n // groups)
    dtypes = {"xp": F32, "q": BF16, "k": F32, "v": F32, "u": F32, "k_b": BF16, "v_heads": F32,
              "kt": F32, "qt_b": BF16, "vt_b": BF16}
    out_specs, out_shape = [], []
    for nm in names:
        if nm in ("kt", "qt_b", "vt_b"):
            spec, shape = t_spec, t_shape
        elif nm == "v_heads":
            spec, shape = row_spec(tm * n_heads, HEAD_LANES), (n * n_heads, HEAD_LANES)
        elif nm == "xp":
            spec, shape = row_spec(tm, d), (n, d)
        else:
            spec, shape = row_spec(tm, width), (n, width)
        out_specs.append(spec)
        out_shape.append(jax.ShapeDtypeStruct(shape, dtypes[nm]))
    outs = pl.pallas_call(
        functools.partial(_inproj_kernel, width=width, names=tuple(names)),
        grid=(n // tm,),
        in_specs=[row_spec(tm, d), mod_spec(0), mod_spec(1),
                  _const_spec((1, d)), _const_spec((1, d)),
                  _const_spec(w_in_bf.shape)],
        out_specs=out_specs,
        out_shape=out_shape,
        compiler_params=pltpu.CompilerParams(
            dimension_semantics=("arbitrary",), vmem_limit_bytes=VMEM_LIMIT),
        name="in_projection",
    )(x2d, mod, mod, ln_g.reshape(1, d), ln_b.reshape(1, d), w_in_bf)
    return dict(zip(names, outs))


def _lambda_value(lq1_ref, lk1_ref, lq2_ref, lk2_ref, lambda_init):
    s1 = jnp.sum(lq1_ref[...] * lk1_ref[...], axis=-1, keepdims=True)
    s2 = jnp.sum(lq2_ref[...] * lk2_ref[...], axis=-1, keepdims=True)
    return jnp.exp(s1) - jnp.exp(s2) + lambda_init


def _sub_ln(o, g, lambda_init):
    ms = jnp.mean(o * o, axis=-1, keepdims=True)
    return o * lax.rsqrt(ms + LN_EPS) * g * (1.0 - lambda_init)


def _prompt_attn_kernel(qi_tab, ki_tab, qt_ref, k_ref, vt_ref, lq1_ref, lk1_ref, lq2_ref, lk2_ref,
                        g_ref, o_ref, qaug_scr, pos_scr, s_scr, m_scr, acc_scr, *,
                        n_heads, tq, tk, lambda_init):
    pair = pl.program_id(1)
    qi = qi_tab[pair]
    ki = ki_tab[pair]

    @pl.when(ki == 0)
    def _init():
        m_scr[...] = jnp.full(m_scr.shape, NEG_BIG, F32)
        acc_scr[...] = jnp.zeros(acc_scr.shape, F32)
        r = lax.broadcasted_iota(jnp.int32, (HEAD_LANES, tq), 0)
        for h in range(n_heads):
            qt = qt_ref[h * HEAD_LANES:(h + 1) * HEAD_LANES, :].astype(F32)
            slope = _alibi_slope(h, n_heads)
            bot = jnp.where(r == 0, ALIBI_POS_SPLIT * slope, jnp.where(r == 1, slope, 0.0))
            for j in range(2):
                own = (r < HEAD_DIM) if j == 0 else (r >= HEAD_DIM)
                top = jnp.where(own, qt, 0.0)
                qaug_scr[2 * h + j] = jnp.concatenate([top, bot], axis=0).astype(BF16)

    def step(on_diagonal):
        kr = lax.broadcasted_iota(jnp.int32, (tk, LANE), 0)
        lane = lax.broadcasted_iota(jnp.int32, (tk, LANE), 1)
        kpos = ki * tk + kr
        pos_hi = kpos // ALIBI_POS_SPLIT
        pos_lo = kpos % ALIBI_POS_SPLIT
        pos_scr[...] = jnp.where(lane == 0, pos_hi, jnp.where(lane == 1, pos_lo, 0)
                                 ).astype(F32).astype(BF16)
        ones_row = jnp.where(lax.broadcasted_iota(jnp.int32, (2 * SUBLANE, tk), 0) == 0,
                             1.0, 0.0).astype(BF16)
        if on_diagonal:
            key = lax.broadcasted_iota(jnp.int32, (tk, tq), 0)
            qry = lax.broadcasted_iota(jnp.int32, (tk, tq), 1)
            visible = key <= qry

        def head_cols(slab):
            return slice((slab // 2) * HEAD_LANES, (slab // 2 + 1) * HEAD_LANES)

        def scores(slab):
            kaug = jnp.concatenate([k_ref[:, head_cols(slab)], pos_scr[...]], axis=1)
            s_scr[slab % 2] = _dot(kaug, qaug_scr[slab])

        def softmax_update(slab):
            s = s_scr[slab % 2]
            if on_diagonal:
                s = jnp.where(visible, s, NEG_BIG)
            m_prev = m_scr[slab]
            m_new = jnp.maximum(m_prev, jnp.max(s, axis=0, keepdims=True))
            alpha = jnp.exp(m_prev - m_new)
            p = jnp.exp(s - m_new).astype(BF16)
            vaug = jnp.concatenate([vt_ref[head_cols(slab), :], ones_row], axis=0)
            acc_scr[slab] = alpha * acc_scr[slab] + _dot(vaug, p)
            m_scr[slab] = m_new

        n_slabs = 2 * n_heads
        scores(0)
        for slab in range(n_slabs):
            if slab + 1 < n_slabs:
                scores(slab + 1)
            softmax_update(slab)

    @pl.when(ki < qi)
    def _below_diagonal():
        step(False)

    @pl.when(ki == qi)
    def _diagonal_and_finalize():
        step(True)
        lam = _lambda_value(lq1_ref, lk1_ref, lq2_ref, lk2_ref, lambda_init)
        def normalised(slab):
            acc = acc_scr[slab]
            return acc[:HEAD_LANES] * (1.0 / acc[HEAD_LANES:HEAD_LANES + 1])

        for h in range(n_heads):
            o = normalised(2 * h) - lam * normalised(2 * h + 1)
            ms = jnp.mean(o * o, axis=0, keepdims=True)
            o = o * lax.rsqrt(ms + LN_EPS) * g_ref[...] * (1.0 - lambda_init)
            o_ref[:, h * HEAD_LANES:(h + 1) * HEAD_LANES] = o.T.astype(o_ref.dtype)


def _prompt_attention(qt_b, k_b, vt_b, lam_params, subln_g, lambda_init, tq=ROW_TILE):
    bsz, width, seq = qt_b.shape
    n_heads = width // HEAD_LANES
    tk = tq
    nq = seq // tq
    assert (seq - 1) // ALIBI_POS_SPLIT < 256, "key position parts must be exact in bf16"
    lq1, lk1, lq2, lk2 = [p.reshape(1, HEAD_DIM) for p in lam_params]
    pairs = [(i, j) for i in range(nq) for j in range(i + 1)]
    qi_tab = jnp.asarray([p[0] for p in pairs], jnp.int32)
    ki_tab = jnp.asarray([p[1] for p in pairs], jnp.int32)
    qt_spec = pl.BlockSpec((None, width, tq), lambda b, s, qi, ki: (b, 0, qi[s]))
    k_spec = pl.BlockSpec((tk, width), lambda b, s, qi, ki: (b * nq + ki[s], 0))
    vt_spec = pl.BlockSpec((None, width, tk), lambda b, s, qi, ki: (b, 0, ki[s]))
    o_spec = pl.BlockSpec((tq, width), lambda b, s, qi, ki: (b * nq + qi[s], 0))
    small = lambda r, w: pl.BlockSpec((r, w), lambda b, s, qi, ki: (0, 0))
    grid_spec = pltpu.PrefetchScalarGridSpec(
        num_scalar_prefetch=2,
        grid=(bsz, len(pairs)),
        in_specs=[qt_spec, k_spec, vt_spec,
                  small(1, HEAD_DIM), small(1, HEAD_DIM), small(1, HEAD_DIM), small(1, HEAD_DIM),
                  small(HEAD_LANES, 1)],
        out_specs=o_spec,
        scratch_shapes=[pltpu.VMEM((2 * n_heads, 2 * HEAD_LANES, tq), BF16),
                        pltpu.VMEM((tk, LANE), BF16),
                        pltpu.VMEM((2, tk, tq), F32),
                        pltpu.VMEM((2 * n_heads, 1, tq), F32),
                        pltpu.VMEM((2 * n_heads, HEAD_LANES + 2 * SUBLANE, tq), F32)],
    )
    return pl.pallas_call(
        functools.partial(_prompt_attn_kernel, n_heads=n_heads, tq=tq, tk=tk,
                          lambda_init=lambda_init),
        grid_spec=grid_spec,
        out_shape=jax.ShapeDtypeStruct((bsz * seq, width), BF16),
        compiler_params=pltpu.CompilerParams(
            dimension_semantics=("arbitrary", "arbitrary"), vmem_limit_bytes=VMEM_LIMIT),
        name="prompt_attention",
    )(qi_tab, ki_tab, qt_b, k_b, vt_b, lq1, lk1, lq2, lk2, subln_g.reshape(HEAD_LANES, 1))


def _sample_attn_tile(q_ref, kn_ref, vn_ref, lq1_ref, lk1_ref, lq2_ref, lk2_ref, g_ref,
                      kt_refs, v_refs, o_ref, s_scr, *, page, n_new, n_heads, lambda_init):
    n_pages = len(kt_refs)
    width = n_heads * HEAD_LANES
    n_maps = 2 * n_heads
    rows = n_new * n_maps
    past = n_pages * page

    q = q_ref[0].astype(F32)
    mp = lax.broadcasted_iota(jnp.int32, (n_maps, width), 0)
    cl = lax.broadcasted_iota(jnp.int32, (n_maps, width), 1)
    map_lanes = (cl // HEAD_DIM) == mp
    qbd32 = jnp.concatenate(
        [jnp.where(map_lanes, jnp.broadcast_to(q[i:i + 1, :], (n_maps, width)), 0.0)
         for i in range(n_new)], axis=0)
    qbd = qbd32.astype(BF16)

    for p in range(n_pages):
        s_scr[:, p * page:(p + 1) * page] = _dot(qbd, kt_refs[p][...].astype(BF16))

    rmap = lax.broadcasted_iota(jnp.int32, (rows, 1), 0) % n_maps
    rqry = lax.broadcasted_iota(jnp.int32, (rows, 1), 0) // n_maps
    slope = jnp.zeros((rows, 1), F32)
    for h in range(n_heads):
        slope = jnp.where(rmap // 2 == h, _alibi_slope(h, n_heads), slope)
    tpos = lax.broadcasted_iota(jnp.int32, (1, past), 1).astype(F32)
    s = s_scr[...] + slope * tpos

    kn = kn_ref[0]
    vn = vn_ref[0]
    s_new = []
    for j in range(n_new):
        sj = jnp.sum(qbd32 * kn[j:j + 1, :], axis=-1, keepdims=True) + slope * float(past + j)
        s_new.append(jnp.where(rqry >= j, sj, NEG_BIG))
    m = jnp.max(s, axis=-1, keepdims=True)
    for sj in s_new:
        m = jnp.maximum(m, sj)
    pr = jnp.exp(s - m)
    l = jnp.sum(pr, axis=-1, keepdims=True)
    prb = pr.astype(BF16)
    p_new = []
    for sj in s_new:
        pj = jnp.exp(sj - m)
        l = l + pj
        p_new.append(pj)
    inv_l = 1.0 / l

    lam = _lambda_value(lq1_ref, lk1_ref, lq2_ref, lk2_ref, lambda_init)
    mrow = lax.broadcasted_iota(jnp.int32, (n_maps, 1), 0)
    for h in range(n_heads):
        cols = slice(h * HEAD_LANES, (h + 1) * HEAD_LANES)
        out = jnp.zeros((rows, HEAD_LANES), F32)
        for j, pj in enumerate(p_new):
            out = out + pj * vn[j:j + 1, cols]
        for p in range(n_pages):
            vh = v_refs[p][pl.ds(h, page, stride=n_heads), :].astype(BF16)
            out = out + _dot(prb[:, p * page:(p + 1) * page], vh)
        out = out * inv_l
        weight = jnp.where(mrow == 2 * h, 1.0, jnp.where(mrow == 2 * h + 1, -lam, 0.0))
        for i in range(n_new):
            o = jnp.sum(out[i * n_maps:(i + 1) * n_maps, :] * weight, axis=0, keepdims=True)
            o_ref[0, i:i + 1, cols] = _sub_ln(o, g_ref[...], lambda_init)
    o_ref[0, n_new:, :] = jnp.zeros((SUBLANE - n_new, width), F32)


def _sample_attn_prompt_conv_kernel(pt_ref, q_ref, kn_ref, vn_ref, lq1_ref, lk1_ref, lq2_ref,
                                    lk2_ref, g_ref, u_ref, cw_ref, cb_ref, clg_ref, clb_ref,
                                    kt_hbm, v_hbm, o_ref, conv_o_ref,
                                    s_scr, buf, wrep, cv_scr, kbuf, vbuf, sem, *,
                                    n_seq, n_pages, page, n_new, n_heads, lambda_init,
                                    tiles_per_seq, tt, taps, halo):
    step = pl.program_id(0)

    def page_copies(seq, slot, pages):
        copies = []
        for p in pages:
            page_id = pt_ref[seq * n_pages + p]
            copies.append(pltpu.make_async_copy(kt_hbm.at[page_id], kbuf.at[slot, p],
                                                sem.at[slot, 0, p]))
            copies.append(pltpu.make_async_copy(v_hbm.at[page_id], vbuf.at[slot, p],
                                                sem.at[slot, 1, p]))
        return copies

    all_pages = range(n_pages)

    @pl.when(step == 0)
    def _prime():
        for ahead in range(PAGES_AHEAD):
            for copy in page_copies(ahead, ahead, all_pages):
                copy.start()

    for slot in range(PAGE_SLOTS):
        seq = step * PAGE_SLOTS + slot
        next_slot = (slot + PAGES_AHEAD) % PAGE_SLOTS
        next_seq = seq + PAGES_AHEAD
        next_seq = jnp.where(next_seq >= n_seq, next_seq - n_seq, next_seq)

        def fetch_next(i, n_blocks, next_seq=next_seq, next_slot=next_slot):
            per_block = -(-n_pages // n_blocks)
            pages = range(i * per_block, min((i + 1) * per_block, n_pages))
            for copy in page_copies(next_seq, next_slot, pages):
                copy.start()

        rows = pl.ds(slot * tt, tt)
        _conv_tile(seq % tiles_per_seq == 0, u_ref.at[rows], cw_ref, cb_ref, clg_ref, clb_ref,
                   conv_o_ref.at[rows], buf, wrep, cv_scr, tt=tt, taps=taps, halo=halo, rc=tt,
                   before_block=fetch_next)
        for copy in page_copies(seq, slot, all_pages):
            copy.wait()
        one = pl.ds(slot, 1)
        _sample_attn_tile(q_ref.at[one], kn_ref.at[one], vn_ref.at[one], lq1_ref, lk1_ref,
                          lq2_ref, lk2_ref, g_ref,
                          [kbuf.at[slot, p] for p in all_pages],
                          [vbuf.at[slot, p] for p in all_pages],
                          o_ref.at[one], s_scr, page=page, n_new=n_new, n_heads=n_heads,
                          lambda_init=lambda_init)

    @pl.when(step == pl.num_programs(0) - 1)
    def _drain():
        for ahead in range(PAGES_AHEAD):
            for copy in page_copies(ahead, ahead, all_pages):
                copy.wait()


def _sample_attention_and_prompt_conv(q8, kn8, vn8, cache_kt, cache_v, page_table, lam_params,
                                      subln_g, n_new, lambda_init,
                                      u, conv_w, conv_b, conv_ln_g, conv_ln_b, seq):
    n_seq, _, width = q8.shape
    n_pages = page_table.shape[1]
    page = cache_kt.shape[2]
    n_heads = width // HEAD_LANES
    assert 2 * n_heads == SUBLANE and n_new <= SUBLANE
    assert n_seq % PAGE_SLOTS == 0 and 0 < PAGES_AHEAD < PAGE_SLOTS
    n_rows, conv_width = u.shape
    taps = conv_w.shape[0]
    halo = -(-(taps - 1) // SUBLANE) * SUBLANE
    tt = n_rows // n_seq
    assert tt * n_seq == n_rows and seq % tt == 0 and tt >= halo and tt % SUBLANE == 0
    n_slab = conv_width // LANE
    lq1, lk1, lq2, lk2 = [p.reshape(1, HEAD_DIM) for p in lam_params]
    seq_spec = pl.BlockSpec((PAGE_SLOTS, SUBLANE, width), lambda s, pt: (s, 0, 0))
    small = lambda r, w: pl.BlockSpec((r, w), lambda s, pt: (0, 0))
    conv_rows = pl.BlockSpec((PAGE_SLOTS * tt, conv_width), lambda s, pt: (s, 0))
    hbm = pl.BlockSpec(memory_space=pl.ANY)

    grid_spec = pltpu.PrefetchScalarGridSpec(
        num_scalar_prefetch=1,
        grid=(n_seq // PAGE_SLOTS,),
        in_specs=[seq_spec, seq_spec, seq_spec,
                  small(1, HEAD_DIM), small(1, HEAD_DIM), small(1, HEAD_DIM), small(1, HEAD_DIM),
                  small(1, HEAD_LANES),
                  conv_rows, small(taps, conv_width), small(1, conv_width),
                  small(1, conv_width), small(1, conv_width), hbm, hbm],
        out_specs=[seq_spec, conv_rows],
        scratch_shapes=[pltpu.VMEM((n_new * 2 * n_heads, n_pages * page), F32),
                        pltpu.VMEM((n_slab, halo + tt, LANE), F32),
                        pltpu.VMEM((taps + 1, n_slab, SUBLANE, LANE), F32),
                        pltpu.VMEM((n_slab, tt, LANE), F32),
                        pltpu.VMEM((PAGE_SLOTS, n_pages) + cache_kt.shape[1:], F32),
                        pltpu.VMEM((PAGE_SLOTS, n_pages) + cache_v.shape[1:], F32),
                        pltpu.SemaphoreType.DMA((PAGE_SLOTS, 2, n_pages))],
    )
    vec = lambda a: a.reshape(1, -1)
    return pl.pallas_call(
        functools.partial(_sample_attn_prompt_conv_kernel, n_seq=n_seq, n_pages=n_pages,
                          page=page, n_new=n_new, n_heads=n_heads, lambda_init=lambda_init,
                          tiles_per_seq=seq // tt, tt=tt, taps=taps, halo=halo),
        grid_spec=grid_spec,
        out_shape=[jax.ShapeDtypeStruct((n_seq, SUBLANE, width), F32),
                   jax.ShapeDtypeStruct((n_rows, conv_width), BF16)],
        compiler_params=pltpu.CompilerParams(
            dimension_semantics=("arbitrary",), vmem_limit_bytes=VMEM_LIMIT),
        name="sample_attention_prompt_conv",
    )(page_table.reshape(-1), q8, kn8, vn8, lq1, lk1, lq2, lk2, vec(subln_g),
      u, conv_w, vec(conv_b), vec(conv_ln_g), vec(conv_ln_b), cache_kt, cache_v)


def _conv_tile(first, u_ref, w_ref, b_ref, lg_ref, lb_ref, o_ref, buf, wrep, cv_scr, *,
               tt, taps, halo, rc, before_block):
    n_slab = buf.shape[0]

    @pl.when(first)
    def _start_of_sequence():
        buf[:, 0:halo, :] = jnp.zeros((n_slab, halo, LANE), F32)
        for s in range(n_slab):
            cols = slice(s * LANE, (s + 1) * LANE)
            wrep[taps, s] = jnp.broadcast_to(b_ref[:, cols], (SUBLANE, LANE))
            for j in range(taps):
                wrep[j, s] = jnp.broadcast_to(w_ref[j:j + 1, cols], (SUBLANE, LANE))

    for s in range(n_slab):
        buf[s, halo:halo + tt, :] = u_ref[:, s * LANE:(s + 1) * LANE]
    base = halo - (taps - 1)
    groups = rc // SUBLANE

    def block_conv(i):
        s = i % n_slab
        r0 = (i // n_slab) * rc
        acc = jnp.broadcast_to(wrep[taps, s][None], (groups, SUBLANE, LANE))
        for phase in range(SUBLANE):
            phase_taps = range(phase, taps, SUBLANE)
            n_groups = groups + len(phase_taps) - 1
            x = buf[s, pl.ds(base + r0 + phase, n_groups * SUBLANE), :].reshape(
                n_groups, SUBLANE, LANE)
            for a, j in enumerate(phase_taps):
                acc = acc + wrep[j, s][None] * x[a:a + groups]
        cv_scr[s, pl.ds(r0, rc), :] = acc.reshape(rc, LANE)

    n_blocks = (tt // rc) * n_slab
    for i in range(n_blocks):
        before_block(i, n_blocks)
        block_conv(i)
    cv = jnp.concatenate([cv_scr[s] for s in range(n_slab)], axis=1)
    o_ref[...] = _silu(_ln(cv, lg_ref[...], lb_ref[...])).astype(o_ref.dtype)
    for s in range(n_slab):
        buf[s, 0:halo, :] = buf[s, tt:tt + halo, :]


def _sample_conv_kernel(full_ref, w_ref, b_ref, lg_ref, lb_ref, o_ref, *, n_new, n_seq, taps):
    for t in range(n_new):
        cv = jnp.broadcast_to(b_ref[...], (n_seq, full_ref.shape[2]))
        for j in range(taps):
            cv = cv + w_ref[j:j + 1, :] * full_ref[t + j]
        o_ref[t] = _silu(_ln(cv, lg_ref[...], lb_ref[...])).astype(o_ref.dtype)


def _sample_conv(full_tm, conv_w, conv_b, ln_g, ln_b, n_new):
    n_time, n_seq, width = full_tm.shape
    taps = conv_w.shape[0]
    return pl.pallas_call(
        functools.partial(_sample_conv_kernel, n_new=n_new, n_seq=n_seq, taps=taps),
        out_shape=jax.ShapeDtypeStruct((n_new, n_seq, width), BF16),
        compiler_params=pltpu.CompilerParams(vmem_limit_bytes=VMEM_LIMIT),
        name="sample_conv",
    )(full_tm, conv_w, conv_b.reshape(1, width), ln_g.reshape(1, width), ln_b.reshape(1, width))


def _ffn_kernel(*refs, has_state, tm, pad, tshift, fc, d_ff, alpha):
    (xp_ref, attn_ref, conv_ref, g1_ref, sh2_ref, sc2_ref, g2_ref,
     wout_ref, l1g_ref, l1b_ref, wup_ref, cw_ref, cb_ref, wdn_ref, l2g_ref, l2b_ref) = refs[:16]
    refs = refs[16:]
    if has_state:
        state_ref, refs = refs[0], refs[1:]
    y_ref, tail_ref, ubuf, acc_ref = refs
    half = attn_ref.shape[1]
    n_slab = fc // LANE

    if has_state:
        hist_ref = state_ref
    else:
        hist_ref = tail_ref

        @pl.when(pl.program_id(1) == 0)
        def _zero_history():
            tail_ref[...] = jnp.zeros(tail_ref.shape, F32)

    def mod_rows(ref):
        m = ref[...]
        reps = 1 if m.shape[0] == 1 else tm // m.shape[0]
        return m if reps == 1 else jnp.concatenate([m] * reps, axis=0)

    mix = _dot(attn_ref[...], wout_ref[0:half, :]) + _dot(conv_ref[...], wout_ref[half:, :])
    x1 = _ln(alpha * xp_ref[...] + mod_rows(g1_ref) * mix, l1g_ref[...], l1b_ref[...])
    h2 = (x1 * (1.0 + mod_rows(sc2_ref)) + mod_rows(sh2_ref)).astype(BF16)

    n_chunks = d_ff // fc

    def up_project(c):
        for part in range(2):
            col = part * d_ff + c * fc
            up = _dot(h2, wup_ref[:, col:col + fc])
            for s in range(n_slab):
                cols = slice(col + s * LANE, col + (s + 1) * LANE)
                slab = ((c % 2) * 2 + part) * n_slab + s
                ubuf[slab, 0:pad, :] = hist_ref[:, cols]
                ubuf[slab, pad:pad + tm, :] = up[:, s * LANE:(s + 1) * LANE]
            tail_ref[:, col:col + fc] = up[tm - pad:, :]

    def conv_half(c, part):
        parts = []
        for s in range(n_slab):
            col = part * d_ff + c * fc + s * LANE
            cols = slice(col, col + LANE)
            slab = ((c % 2) * 2 + part) * n_slab + s
            parts.append(cw_ref[0:1, cols] * ubuf[slab, pad - 2 * tshift:pad - 2 * tshift + tm, :]
                         + cw_ref[1:2, cols] * ubuf[slab, pad - tshift:pad - tshift + tm, :]
                         + cw_ref[2:3, cols] * ubuf[slab, pad:pad + tm, :] + cb_ref[:, cols])
        return jnp.concatenate(parts, axis=1)

    up_project(0)
    for c in range(n_chunks):
        if c + 1 < n_chunks:
            up_project(c + 1)
        g = (_silu(conv_half(c, 0)) * conv_half(c, 1)).astype(BF16)
        contrib = _dot(g, wdn_ref[c * fc:(c + 1) * fc, :])
        if c == 0:
            acc_ref[...] = contrib
        else:
            acc_ref[...] += contrib

    y_ref[...] = _ln(alpha * x1 + mod_rows(g2_ref) * acc_ref[...], l2g_ref[...], l2b_ref[...])


def _ffn(xp2d, attn, conv, mod, w_out_bf, ln1, w_up_bf, ffn_conv_w, ffn_conv_b,
         w_down_bf, ln2, *, groups, tiles, tm, pad, tshift, state, alpha, fc=MXU_TILE):
    n, d = xp2d.shape
    half = attn.shape[1]
    d_ff = w_down_bf.shape[0]
    r = mod.shape[1]
    has_state = state is not None
    assert pad == 2 * tshift or not has_state
    assert tm >= pad
    row = lambda w: pl.BlockSpec((tm, w), lambda b, t: (b * tiles + t, 0))
    mod_spec = lambda comp: pl.BlockSpec((None, r, d), lambda b, t: (b, 0, comp))
    vec = lambda a: a.reshape(1, -1)
    args = [xp2d, attn, conv, mod, mod, mod, mod,
            w_out_bf, vec(ln1[0]), vec(ln1[1]), w_up_bf, ffn_conv_w, vec(ffn_conv_b),
            w_down_bf, vec(ln2[0]), vec(ln2[1])]
    in_specs = [row(d), row(half), row(half), mod_spec(2), mod_spec(3), mod_spec(4), mod_spec(5),
                _const_spec(w_out_bf.shape), _const_spec((1, d)), _const_spec((1, d)),
                _const_spec(w_up_bf.shape), _const_spec(ffn_conv_w.shape),
                _const_spec((1, 2 * d_ff)), _const_spec(w_down_bf.shape),
                _const_spec((1, d)), _const_spec((1, d))]
    if has_state:
        args.append(state)
        in_specs.append(_const_spec(state.shape))
    y, tail = pl.pallas_call(
        functools.partial(_ffn_kernel, has_state=has_state, tm=tm, pad=pad,
                          tshift=tshift, fc=fc, d_ff=d_ff, alpha=alpha),
        grid=(groups, tiles),
        in_specs=in_specs,
        out_specs=[row(d), pl.BlockSpec((None, pad, 2 * d_ff), lambda b, t: (b, 0, 0))],
        out_shape=[jax.ShapeDtypeStruct((n, d), F32),
                   jax.ShapeDtypeStruct((groups, pad, 2 * d_ff), F32)],
        scratch_shapes=[pltpu.VMEM((4 * (fc // LANE), pad + tm, LANE), F32),
                        pltpu.VMEM((tm, d), F32)],
        compiler_params=pltpu.CompilerParams(
            dimension_semantics=("arbitrary", "arbitrary"), vmem_limit_bytes=VMEM_LIMIT),
        name="ffn",
    )(*args)
    return y, tail


def kernel(x_prompt, x_sample, c_prompt, c_sample, cache_k, cache_v, page_table, state_conv, state_ffn, ln_emb_g, ln_emb_b, w_ada, b_ada, w_in, lambda_q1, lambda_k1, lambda_q2, lambda_k2, subln_g, conv_w, conv_b, conv_ln_g, conv_ln_b, w_out, ln1_g, ln1_b, w_up, ffn_conv_w, ffn_conv_b, w_down, ln2_g, ln2_b):
    bsz, seq, d = x_prompt.shape
    n_seq, n_new, _ = x_sample.shape
    depth = w_ada.shape[0]
    assert depth == 1, "the prompt/sample activations are threaded for a single layer"
    n_phys, page = cache_k.shape[1], cache_k.shape[2]
    width = cache_k.shape[3] * cache_k.shape[4]
    d_ff = w_down.shape[1]
    conv_taps = conv_w.shape[1]
    ffn_taps = ffn_conv_w.shape[1]
    assert ffn_taps == 3
    alpha = (2 * depth) ** 0.25
    tm = ROW_TILE
    n_rows_s = n_seq * n_new
    n_kh = width // HEAD_DIM
    n_vh = width // HEAD_LANES

    l = 0
    lambda_init = 0.8 - 0.6 * math.exp(-0.3 * l)
    lam_params = (lambda_q1[l], lambda_k1[l], lambda_q2[l], lambda_k2[l])
    w_in_bf = w_in[l].astype(BF16)
    w_out_bf = w_out[l].astype(BF16)
    w_up_bf = w_up[l].astype(BF16)
    w_down_bf = w_down[l].astype(BF16)

    c_rows = bsz + n_seq
    c_all = jnp.concatenate([c_prompt, c_sample], axis=0)
    c_all = jnp.pad(c_all, ((0, -c_rows % (2 * SUBLANE)), (0, 0)))
    mod = _modulation(c_all, w_ada[l], b_ada[l])
    mod_p = mod[:bsz].reshape(bsz, 1, 6 * d)
    mod_s = mod[bsz:c_rows]

    pp = _in_projection(x_prompt.reshape(bsz * seq, d), mod_p, ln_emb_g, ln_emb_b, w_in_bf,
                        groups=bsz, tm=tm,
                        names=("xp", "qt_b", "k_b", "kt", "v_heads", "vt_b", "u"))
    u = pp["u"]

    xs_tm = x_sample.swapaxes(0, 1).reshape(n_rows_s, d)
    ps = _in_projection(xs_tm, mod_s[None], ln_emb_g, ln_emb_b, w_in_bf, groups=n_new, tm=n_seq,
                        names=("xp", "q", "k", "kt", "v", "u"))
    qs, ks, kts, vs, us = ps["q"], ps["k"], ps["kt"], ps["v"], ps["u"]
    to_sm = lambda a: a.reshape(n_new, n_seq, -1).swapaxes(0, 1)
    pad8 = lambda a: jnp.pad(to_sm(a), ((0, 0), (0, SUBLANE - n_new), (0, 0)))
    cache_kt = cache_k[l].transpose(0, 2, 3, 1).reshape(n_phys, width, page)
    cache_vr = cache_v[l].reshape(n_phys, page * n_vh, HEAD_LANES)
    attn_s8, conv = _sample_attention_and_prompt_conv(
        pad8(qs), pad8(ks), pad8(vs), cache_kt, cache_vr, page_table, lam_params, subln_g[l],
        n_new, lambda_init, u, conv_w[l], conv_b[l], conv_ln_g[l], conv_ln_b[l], seq)
    attn_s = attn_s8[:, :n_new, :].astype(BF16).swapaxes(0, 1).reshape(n_rows_s, width)

    attn = _prompt_attention(pp["qt_b"], pp["k_b"], pp["vt_b"], lam_params, subln_g[l],
                             lambda_init)
    y_p, tail_p = _ffn(pp["xp"], attn, conv, mod_p, w_out_bf,
                       (ln1_g[l], ln1_b[l]), w_up_bf, ffn_conv_w[l], ffn_conv_b[l], w_down_bf,
                       (ln2_g[l], ln2_b[l]), groups=bsz, tiles=seq // tm, tm=tm,
                       pad=SUBLANE, tshift=1, state=None, alpha=alpha)
    y_prompt = y_p.reshape(bsz, seq, d)
    k_prompt = pp["kt"].reshape(bsz, n_kh, HEAD_DIM, seq).transpose(0, 3, 1, 2)[None]
    v_prompt = pp["v_heads"].reshape(1, bsz, seq, n_vh, HEAD_LANES)
    conv_prompt = u.reshape(bsz, seq, -1)[None, :, seq - (conv_taps - 1):, :]
    ffn_prompt = tail_p[None, :, SUBLANE - (ffn_taps - 1):, :]

    full_tm = jnp.concatenate([state_conv[l].swapaxes(0, 1), us.reshape(n_new, n_seq, -1)], axis=0)
    conv_s = _sample_conv(full_tm, conv_w[l], conv_b[l], conv_ln_g[l], conv_ln_b[l],
                          n_new).reshape(n_rows_s, -1)
    state_tm = state_ffn[l].swapaxes(0, 1).reshape((ffn_taps - 1) * n_seq, 2 * d_ff)
    y_s, tail_s = _ffn(ps["xp"], attn_s, conv_s, mod_s[None],
                       w_out_bf, (ln1_g[l], ln1_b[l]), w_up_bf,
                       ffn_conv_w[l], ffn_conv_b[l], w_down_bf, (ln2_g[l], ln2_b[l]),
                       groups=1, tiles=1, tm=n_rows_s, pad=(ffn_taps - 1) * n_seq,
                       tshift=n_seq, state=state_tm, alpha=alpha)
    y_sample = y_s.reshape(n_new, n_seq, d).swapaxes(0, 1)
    k_sample = kts.reshape(n_new, n_kh, HEAD_DIM, n_seq).transpose(3, 0, 1, 2)[None]
    v_sample = to_sm(vs).reshape(1, n_seq, n_new, n_vh, HEAD_LANES)
    conv_sample = full_tm[n_new:].swapaxes(0, 1)[None]
    ffn_sample = tail_s.reshape(ffn_taps - 1, n_seq, 2 * d_ff).swapaxes(0, 1)[None]

    return (y_prompt, y_sample, k_prompt, v_prompt, conv_prompt, ffn_prompt,
            k_sample, v_sample, conv_sample, ffn_sample)
```

```python
import functools
import math

import jax
import jax.numpy as jnp
from jax import lax
from jax.experimental import pallas as pl
from jax.experimental.pallas import tpu as pltpu

F32 = jnp.float32
BF16 = jnp.bfloat16

LN_EPS = 1e-5
HEAD_DIM = 64
HEAD_LANES = 2 * HEAD_DIM
LANE = 128
SUBLANE = 8
VMEM_LIMIT = 56 * 1024 * 1024
NEG_BIG = -1e30
ALIBI_POS_SPLIT = 64
PAGE_SLOTS = 4
PAGES_AHEAD = 2
MXU_TILE = 256
ROW_TILE = 2 * MXU_TILE

def _ln(x, g, b):
    mu = jnp.mean(x, axis=-1, keepdims=True)
    xc = x - mu
    var = jnp.mean(xc * xc, axis=-1, keepdims=True)
    return xc * lax.rsqrt(var + LN_EPS) * g + b


def _silu(x):
    return x * jax.nn.sigmoid(x)


def _dot(a, b):
    return jnp.dot(a, b, preferred_element_type=F32)


def _const_spec(shape):
    nd = len(shape)
    return pl.BlockSpec(shape, lambda *_: (0,) * nd, pipeline_mode=pl.Buffered(1))


def _alibi_slope(h, n_heads):
    return 2.0 ** (-8.0 * (h + 1) / n_heads)


def _split_bf16(x):
    hi = x.astype(BF16)
    return hi, (x - hi.astype(F32)).astype(BF16)


def _mod_kernel(c_ref, w_ref, b_ref, o_ref):
    a_hi, a_lo = _split_bf16(_silu(c_ref[...]))
    w_hi, w_lo = _split_bf16(w_ref[...])
    o_ref[...] = _dot(a_hi, w_hi) + (_dot(a_hi, w_lo) + _dot(a_lo, w_hi)) + b_ref[...]


def _modulation(c_all, w_ada, b_ada):
    rows, d = c_all.shape
    n_out = w_ada.shape[1]
    return pl.pallas_call(
        _mod_kernel,
        grid=(n_out // d,),
        in_specs=[pl.BlockSpec((rows, d), lambda j: (0, 0)),
                  pl.BlockSpec((d, d), lambda j: (0, j)),
                  pl.BlockSpec((1, d), lambda j: (0, j))],
        out_specs=pl.BlockSpec((rows, d), lambda j: (0, j)),
        out_shape=jax.ShapeDtypeStruct((rows, n_out), F32),
        compiler_params=pltpu.CompilerParams(vmem_limit_bytes=VMEM_LIMIT),
        name="modulation",
    )(c_all, w_ada, b_ada.reshape(1, n_out))


def _inproj_kernel(x_ref, sh_ref, sc_ref, lg_ref, lb_ref, w_ref, *out_refs, width, names):
    out = dict(zip(names, out_refs))
    tm = x_ref.shape[0]
    xp = _ln(x_ref[...], lg_ref[...], lb_ref[...])
    h = (xp * (1.0 + sc_ref[...]) + sh_ref[...]).astype(BF16)

    def proj(i):
        return _dot(h, w_ref[:, i * width:(i + 1) * width])

    def emit(name, value):
        if name in out:
            out[name][...] = value.astype(out[name].dtype)

    emit("xp", xp)
    q = proj(0) * (HEAD_DIM ** -0.5)
    emit("q", q)
    if "qt_b" in out:
        emit("qt_b", q.T)
    k = proj(1)
    emit("k", k)
    emit("k_b", k)
    emit("kt", k.T)
    v = proj(2)
    emit("v", v)
    if "vt_b" in out:
        emit("vt_b", v.T)
    if "v_heads" in out:
        n_heads = width // HEAD_LANES
        for hd in range(n_heads):
            out["v_heads"][pl.ds(hd, tm, stride=n_heads), :] = (
                v[:, hd * HEAD_LANES:(hd + 1) * HEAD_LANES])
    emit("u", proj(3) * jax.nn.sigmoid(proj(4)))


def _in_projection(x2d, mod, ln_g, ln_b, w_in_bf, *, groups, tm, names):
    n, d = x2d.shape
    width = w_in_bf.shape[1] // 5
    n_heads = width // HEAD_LANES
    per_group = n // groups // tm
    mod_groups, r = mod.shape[0], mod.shape[1]

    def mod_spec(comp):
        return pl.BlockSpec((None, r, d),
                            lambda i: ((i // per_group) % mod_groups, 0, comp))

    row_spec = lambda rows, w: pl.BlockSpec((rows, w), lambda i: (i, 0))
    t_spec = pl.BlockSpec((None, width, tm), lambda i: (i // per_group, 0, i % per_group))
    t_shape = (groups, width, n // groups)
    dtypes = {"xp": F32, "q": BF16, "k": F32, "v": F32, "u": F32, "k_b": BF16, "v_heads": F32,
              "kt": F32, "qt_b": BF16, "vt_b": BF16}
    out_specs, out_shape = [], []
    for nm in names:
        if nm in ("kt", "qt_b", "vt_b"):
            spec, shape = t_spec, t_shape
        elif nm == "v_heads":
            spec, shape = row_spec(tm * n_heads, HEAD_LANES), (n * n_heads, HEAD_LANES)
        elif nm == "xp":
            spec, shape = row_spec(tm, d), (n, d)
        else:
            spec, shape = row_spec(tm, width), (n, width)
        out_specs.append(spec)
        out_shape.append(jax.ShapeDtypeStruct(shape, dtypes[nm]))
    outs = pl.pallas_call(
        functools.partial(_inproj_kernel, width=width, names=tuple(names)),
        grid=(n // tm,),
        in_specs=[row_spec(tm, d), mod_spec(0), mod_spec(1),
                  _const_spec((1, d)), _const_spec((1, d)),
                  _const_spec(w_in_bf.shape)],
        out_specs=out_specs,
        out_shape=out_shape,
        compiler_params=pltpu.CompilerParams(
            dimension_semantics=("arbitrary",), vmem_limit_bytes=VMEM_LIMIT),
        name="in_projection",
    )(x2d, mod, mod, ln_g.reshape(1, d), ln_b.reshape(1, d), w_in_bf)
    return dict(zip(names, outs))


def _lambda_value(lq1_ref, lk1_ref, lq2_ref, lk2_ref, lambda_init):
    s1 = jnp.sum(lq1_ref[...] * lk1_ref[...], axis=-1, keepdims=True)
    s2 = jnp.sum(lq2_ref[...] * lk2_ref[...], axis=-1, keepdims=True)
    return jnp.exp(s1) - jnp.exp(s2) + lambda_init


def _sub_ln(o, g, lambda_init):
    ms = jnp.mean(o * o, axis=-1, keepdims=True)
    return o * lax.rsqrt(ms + LN_EPS) * g * (1.0 - lambda_init)


def _prompt_attn_kernel(qi_tab, ki_tab, qt_ref, k_ref, vt_ref, lq1_ref, lk1_ref, lq2_ref, lk2_ref,
                        g_ref, o_ref, qaug_scr, pos_scr, s_scr, m_scr, acc_scr, *,
                        n_heads, tq, tk, lambda_init):
    pair = pl.program_id(1)
    qi = qi_tab[pair]
    ki = ki_tab[pair]

    @pl.when(ki == 0)
    def _init():
        m_scr[...] = jnp.full(m_scr.shape, NEG_BIG, F32)
        acc_scr[...] = jnp.zeros(acc_scr.shape, F32)
        r = lax.broadcasted_iota(jnp.int32, (HEAD_LANES, tq), 0)
        for h in range(n_heads):
            qt = qt_ref[h * HEAD_LANES:(h + 1) * HEAD_LANES, :].astype(F32)
            slope = _alibi_slope(h, n_heads)
            bot = jnp.where(r == 0, ALIBI_POS_SPLIT * slope, jnp.where(r == 1, slope, 0.0))
            for j in range(2):
                own = (r < HEAD_DIM) if j == 0 else (r >= HEAD_DIM)
                top = jnp.where(own, qt, 0.0)
                qaug_scr[2 * h + j] = jnp.concatenate([top, bot], axis=0).astype(BF16)

    def step(on_diagonal):
        kr = lax.broadcasted_iota(jnp.int32, (tk, LANE), 0)
        lane = lax.broadcasted_iota(jnp.int32, (tk, LANE), 1)
        kpos = ki * tk + kr
        pos_hi = kpos // ALIBI_POS_SPLIT
        pos_lo = kpos % ALIBI_POS_SPLIT
        pos_scr[...] = jnp.where(lane == 0, pos_hi, jnp.where(lane == 1, pos_lo, 0)
                                 ).astype(F32).astype(BF16)
        ones_row = jnp.where(lax.broadcasted_iota(jnp.int32, (2 * SUBLANE, tk), 0) == 0,
                             1.0, 0.0).astype(BF16)
        if on_diagonal:
            key = lax.broadcasted_iota(jnp.int32, (tk, tq), 0)
            qry = lax.broadcasted_iota(jnp.int32, (tk, tq), 1)
            visible = key <= qry

        def head_cols(slab):
            return slice((slab // 2) * HEAD_LANES, (slab // 2 + 1) * HEAD_LANES)

        def scores(slab):
            kaug = jnp.concatenate([k_ref[:, head_cols(slab)], pos_scr[...]], axis=1)
            s_scr[slab % 2] = _dot(kaug, qaug_scr[slab])

        def softmax_update(slab):
            s = s_scr[slab % 2]
            if on_diagonal:
                s = jnp.where(visible, s, NEG_BIG)
            m_prev = m_scr[slab]
            m_new = jnp.maximum(m_prev, jnp.max(s, axis=0, keepdims=True))
            alpha = jnp.exp(m_prev - m_new)
            p = jnp.exp(s - m_new).astype(BF16)
            vaug = jnp.concatenate([vt_ref[head_cols(slab), :], ones_row], axis=0)
            acc_scr[slab] = alpha * acc_scr[slab] + _dot(vaug, p)
            m_scr[slab] = m_new

        n_slabs = 2 * n_heads
        scores(0)
        for slab in range(n_slabs):
            if slab + 1 < n_slabs:
                scores(slab + 1)
            softmax_update(slab)

    @pl.when(ki < qi)
    def _below_diagonal():
        step(False)

    @pl.when(ki == qi)
    def _diagonal_and_finalize():
        step(True)
        lam = _lambda_value(lq1_ref, lk1_ref, lq2_ref, lk2_ref, lambda_init)
        def normalised(slab):
            acc = acc_scr[slab]
            return acc[:HEAD_LANES] * (1.0 / acc[HEAD_LANES:HEAD_LANES + 1])

        for h in range(n_heads):
            o = normalised(2 * h) - lam * normalised(2 * h + 1)
            ms = jnp.mean(o * o, axis=0, keepdims=True)
            o = o * lax.rsqrt(ms + LN_EPS) * g_ref[...] * (1.0 - lambda_init)
            o_ref[:, h * HEAD_LANES:(h + 1) * HEAD_LANES] = o.T.astype(o_ref.dtype)


def _prompt_attention(qt_b, k_b, vt_b, lam_params, subln_g, lambda_init, tq=ROW_TILE):
    bsz, width, seq = qt_b.shape
    n_heads = width // HEAD_LANES
    tk = tq
    nq = seq // tq
    assert (seq - 1) // ALIBI_POS_SPLIT < 256, "key position parts must be exact in bf16"
    lq1, lk1, lq2, lk2 = [p.reshape(1, HEAD_DIM) for p in lam_params]
    pairs = [(i, j) for i in range(nq) for j in range(i + 1)]
    qi_tab = jnp.asarray([p[0] for p in pairs], jnp.int32)
    ki_tab = jnp.asarray([p[1] for p in pairs], jnp.int32)
    qt_spec = pl.BlockSpec((None, width, tq), lambda b, s, qi, ki: (b, 0, qi[s]))
    k_spec = pl.BlockSpec((tk, width), lambda b, s, qi, ki: (b * nq + ki[s], 0))
    vt_spec = pl.BlockSpec((None, width, tk), lambda b, s, qi, ki: (b, 0, ki[s]))
    o_spec = pl.BlockSpec((tq, width), lambda b, s, qi, ki: (b * nq + qi[s], 0))
    small = lambda r, w: pl.BlockSpec((r, w), lambda b, s, qi, ki: (0, 0))
    grid_spec = pltpu.PrefetchScalarGridSpec(
        num_scalar_prefetch=2,
        grid=(bsz, len(pairs)),
        in_specs=[qt_spec, k_spec, vt_spec,
                  small(1, HEAD_DIM), small(1, HEAD_DIM), small(1, HEAD_DIM), small(1, HEAD_DIM),
                  small(HEAD_LANES, 1)],
        out_specs=o_spec,
        scratch_shapes=[pltpu.VMEM((2 * n_heads, 2 * HEAD_LANES, tq), BF16),
                        pltpu.VMEM((tk, LANE), BF16),
                        pltpu.VMEM((2, tk, tq), F32),
                        pltpu.VMEM((2 * n_heads, 1, tq), F32),
                        pltpu.VMEM((2 * n_heads, HEAD_LANES + 2 * SUBLANE, tq), F32)],
    )
    return pl.pallas_call(
        functools.partial(_prompt_attn_kernel, n_heads=n_heads, tq=tq, tk=tk,
                          lambda_init=lambda_init),
        grid_spec=grid_spec,
        out_shape=jax.ShapeDtypeStruct((bsz * seq, width), BF16),
        compiler_params=pltpu.CompilerParams(
            dimension_semantics=("arbitrary", "arbitrary"), vmem_limit_bytes=VMEM_LIMIT),
        name="prompt_attention",
    )(qi_tab, ki_tab, qt_b, k_b, vt_b, lq1, lk1, lq2, lk2, subln_g.reshape(HEAD_LANES, 1))


def _sample_attn_tile(q_ref, kn_ref, vn_ref, lq1_ref, lk1_ref, lq2_ref, lk2_ref, g_ref,
                      kt_refs, v_refs, o_ref, s_scr, *, page, n_new, n_heads, lambda_init):
    n_pages = len(kt_refs)
    width = n_heads * HEAD_LANES
    n_maps = 2 * n_heads
    rows = n_new * n_maps
    past = n_pages * page

    q = q_ref[0].astype(F32)
    mp = lax.broadcasted_iota(jnp.int32, (n_maps, width), 0)
    cl = lax.broadcasted_iota(jnp.int32, (n_maps, width), 1)
    map_lanes = (cl // HEAD_DIM) == mp
    qbd32 = jnp.concatenate(
        [jnp.where(map_lanes, jnp.broadcast_to(q[i:i + 1, :], (n_maps, width)), 0.0)
         for i in range(n_new)], axis=0)
    qbd = qbd32.astype(BF16)

    for p in range(n_pages):
        s_scr[:, p * page:(p + 1) * page] = _dot(qbd, kt_refs[p][...].astype(BF16))

    rmap = lax.broadcasted_iota(jnp.int32, (rows, 1), 0) % n_maps
    rqry = lax.broadcasted_iota(jnp.int32, (rows, 1), 0) // n_maps
    slope = jnp.zeros((rows, 1), F32)
    for h in range(n_heads):
        slope = jnp.where(rmap // 2 == h, _alibi_slope(h, n_heads), slope)
    tpos = lax.broadcasted_iota(jnp.int32, (1, past), 1).astype(F32)
    s = s_scr[...] + slope * tpos

    kn = kn_ref[0]
    vn = vn_ref[0]
    s_new = []
    for j in range(n_new):
        sj = jnp.sum(qbd32 * kn[j:j + 1, :], axis=-1, keepdims=True) + slope * float(past + j)
        s_new.append(jnp.where(rqry >= j, sj, NEG_BIG))
    m = jnp.max(s, axis=-1, keepdims=True)
    for sj in s_new:
        m = jnp.maximum(m, sj)
    pr = jnp.exp(s - m)
    l = jnp.sum(pr, axis=-1, keepdims=True)
    prb = pr.astype(BF16)
    p_new = []
    for sj in s_new:
        pj = jnp.exp(sj - m)
        l = l + pj
        p_new.append(pj)
    inv_l = 1.0 / l

    lam = _lambda_value(lq1_ref, lk1_ref, lq2_ref, lk2_ref, lambda_init)
    mrow = lax.broadcasted_iota(jnp.int32, (n_maps, 1), 0)
    for h in range(n_heads):
        cols = slice(h * HEAD_LANES, (h + 1) * HEAD_LANES)
        out = jnp.zeros((rows, HEAD_LANES), F32)
        for j, pj in enumerate(p_new):
            out = out + pj * vn[j:j + 1, cols]
        vh = jnp.concatenate([v_refs[p][pl.ds(h, page, stride=n_heads), :].astype(BF16)
                              for p in range(n_pages)], axis=0)
        out = out + _dot(prb, vh)
        out = out * inv_l
        weight = jnp.where(mrow == 2 * h, 1.0, jnp.where(mrow == 2 * h + 1, -lam, 0.0))
        for i in range(n_new):
            o = jnp.sum(out[i * n_maps:(i + 1) * n_maps, :] * weight, axis=0, keepdims=True)
            o_ref[0, i:i + 1, cols] = _sub_ln(o, g_ref[...], lambda_init)
    o_ref[0, n_new:, :] = jnp.zeros((SUBLANE - n_new, width), F32)


def _sample_attn_prompt_conv_kernel(pt_ref, q_ref, kn_ref, vn_ref, lq1_ref, lk1_ref, lq2_ref,
                                    lk2_ref, g_ref, u_ref, cw_ref, cb_ref, clg_ref, clb_ref,
                                    kt_hbm, v_hbm, o_ref, conv_o_ref,
                                    s_scr, buf, wrep, cv_scr, kbuf, vbuf, sem, *,
                                    n_seq, n_pages, page, n_new, n_heads, lambda_init,
                                    tiles_per_seq, tt, taps, halo):
    step = pl.program_id(0)

    def page_copies(seq, slot, pages):
        copies = []
        for p in pages:
            page_id = pt_ref[seq * n_pages + p]
            copies.append(pltpu.make_async_copy(kt_hbm.at[page_id], kbuf.at[slot, p],
                                                sem.at[slot, 0, p]))
            copies.append(pltpu.make_async_copy(v_hbm.at[page_id], vbuf.at[slot, p],
                                                sem.at[slot, 1, p]))
        return copies

    all_pages = range(n_pages)

    @pl.when(step == 0)
    def _prime():
        for ahead in range(PAGES_AHEAD):
            for copy in page_copies(ahead, ahead, all_pages):
                copy.start()

    for slot in range(PAGE_SLOTS):
        seq = step * PAGE_SLOTS + slot
        next_slot = (slot + PAGES_AHEAD) % PAGE_SLOTS
        next_seq = seq + PAGES_AHEAD
        next_seq = jnp.where(next_seq >= n_seq, next_seq - n_seq, next_seq)

        def fetch_next(i, n_blocks, next_seq=next_seq, next_slot=next_slot):
            per_block = -(-n_pages // n_blocks)
            pages = range(i * per_block, min((i + 1) * per_block, n_pages))
            for copy in page_copies(next_seq, next_slot, pages):
                copy.start()

        rows = pl.ds(slot * tt, tt)
        _conv_tile(seq % tiles_per_seq == 0, u_ref.at[rows], cw_ref, cb_ref, clg_ref, clb_ref,
                   conv_o_ref.at[rows], buf, wrep, cv_scr, tt=tt, taps=taps, halo=halo, rc=tt,
                   before_block=fetch_next)
        for copy in page_copies(seq, slot, all_pages):
            copy.wait()
        one = pl.ds(slot, 1)
        _sample_attn_tile(q_ref.at[one], kn_ref.at[one], vn_ref.at[one], lq1_ref, lk1_ref,
                          lq2_ref, lk2_ref, g_ref,
                          [kbuf.at[slot, p] for p in all_pages],
                          [vbuf.at[slot, p] for p in all_pages],
                          o_ref.at[one], s_scr, page=page, n_new=n_new, n_heads=n_heads,
                          lambda_init=lambda_init)

    @pl.when(step == pl.num_programs(0) - 1)
    def _drain():
        for ahead in range(PAGES_AHEAD):
            for copy in page_copies(ahead, ahead, all_pages):
                copy.wait()


def _sample_attention_and_prompt_conv(q8, kn8, vn8, cache_kt, cache_v, page_table, lam_params,
                                      subln_g, n_new, lambda_init,
                                      u, conv_w, conv_b, conv_ln_g, conv_ln_b, seq):
    n_seq, _, width = q8.shape
    n_pages = page_table.shape[1]
    page = cache_kt.shape[2]
    n_heads = width // HEAD_LANES
    assert 2 * n_heads == SUBLANE and n_new <= SUBLANE
    assert n_seq % PAGE_SLOTS == 0 and 0 < PAGES_AHEAD < PAGE_SLOTS
    n_rows, conv_width = u.shape
    taps = conv_w.shape[0]
    halo = -(-(taps - 1) // SUBLANE) * SUBLANE
    tt = n_rows // n_seq
    assert tt * n_seq == n_rows and seq % tt == 0 and tt >= halo and tt % SUBLANE == 0
    n_slab = conv_width // LANE
    lq1, lk1, lq2, lk2 = [p.reshape(1, HEAD_DIM) for p in lam_params]
    seq_spec = pl.BlockSpec((PAGE_SLOTS, SUBLANE, width), lambda s, pt: (s, 0, 0))
    small = lambda r, w: pl.BlockSpec((r, w), lambda s, pt: (0, 0))
    conv_rows = pl.BlockSpec((PAGE_SLOTS * tt, conv_width), lambda s, pt: (s, 0))
    hbm = pl.BlockSpec(memory_space=pl.ANY)

    grid_spec = pltpu.PrefetchScalarGridSpec(
        num_scalar_prefetch=1,
        grid=(n_seq // PAGE_SLOTS,),
        in_specs=[seq_spec, seq_spec, seq_spec,
                  small(1, HEAD_DIM), small(1, HEAD_DIM), small(1, HEAD_DIM), small(1, HEAD_DIM),
                  small(1, HEAD_LANES),
                  conv_rows, small(taps, conv_width), small(1, conv_width),
                  small(1, conv_width), small(1, conv_width), hbm, hbm],
        out_specs=[seq_spec, conv_rows],
        scratch_shapes=[pltpu.VMEM((n_new * 2 * n_heads, n_pages * page), F32),
                        pltpu.VMEM((n_slab, halo + tt, LANE), F32),
                        pltpu.VMEM((taps + 1, n_slab, SUBLANE, LANE), F32),
                        pltpu.VMEM((n_slab, tt, LANE), F32),
                        pltpu.VMEM((PAGE_SLOTS, n_pages) + cache_kt.shape[1:], F32),
                        pltpu.VMEM((PAGE_SLOTS, n_pages) + cache_v.shape[1:], F32),
                        pltpu.SemaphoreType.DMA((PAGE_SLOTS, 2, n_pages))],
    )
    vec = lambda a: a.reshape(1, -1)
    return pl.pallas_call(
        functools.partial(_sample_attn_prompt_conv_kernel, n_seq=n_seq, n_pages=n_pages,
                          page=page, n_new=n_new, n_heads=n_heads, lambda_init=lambda_init,
                          tiles_per_seq=seq // tt, tt=tt, taps=taps, halo=halo),
        grid_spec=grid_spec,
        out_shape=[jax.ShapeDtypeStruct((n_seq, SUBLANE, width), F32),
                   jax.ShapeDtypeStruct((n_rows, conv_width), BF16)],
        compiler_params=pltpu.CompilerParams(
            dimension_semantics=("arbitrary",), vmem_limit_bytes=VMEM_LIMIT),
        name="sample_attention_prompt_conv",
    )(page_table.reshape(-1), q8, kn8, vn8, lq1, lk1, lq2, lk2, vec(subln_g),
      u, conv_w, vec(conv_b), vec(conv_ln_g), vec(conv_ln_b), cache_kt, cache_v)


def _conv_tile(first, u_ref, w_ref, b_ref, lg_ref, lb_ref, o_ref, buf, wrep, cv_scr, *,
               tt, taps, halo, rc, before_block):
    n_slab = buf.shape[0]

    @pl.when(first)
    def _start_of_sequence():
        buf[:, 0:halo, :] = jnp.zeros((n_slab, halo, LANE), F32)
        for s in range(n_slab):
            cols = slice(s * LANE, (s + 1) * LANE)
            wrep[taps, s] = jnp.broadcast_to(b_ref[:, cols], (SUBLANE, LANE))
            for j in range(taps):
                wrep[j, s] = jnp.broadcast_to(w_ref[j:j + 1, cols], (SUBLANE, LANE))

    for s in range(n_slab):
        buf[s, halo:halo + tt, :] = u_ref[:, s * LANE:(s + 1) * LANE]
    base = halo - (taps - 1)
    groups = rc // SUBLANE

    def block_conv(i):
        s = i % n_slab
        r0 = (i // n_slab) * rc
        acc = jnp.broadcast_to(wrep[taps, s][None], (groups, SUBLANE, LANE))
        for phase in range(SUBLANE):
            phase_taps = range(phase, taps, SUBLANE)
            n_groups = groups + len(phase_taps) - 1
            x = buf[s, pl.ds(base + r0 + phase, n_groups * SUBLANE), :].reshape(
                n_groups, SUBLANE, LANE)
            for a, j in enumerate(phase_taps):
                acc = acc + wrep[j, s][None] * x[a:a + groups]
        cv_scr[s, pl.ds(r0, rc), :] = acc.reshape(rc, LANE)

    n_blocks = (tt // rc) * n_slab
    for i in range(n_blocks):
        before_block(i, n_blocks)
        block_conv(i)
    cv = jnp.concatenate([cv_scr[s] for s in range(n_slab)], axis=1)
    o_ref[...] = _silu(_ln(cv, lg_ref[...], lb_ref[...])).astype(o_ref.dtype)
    for s in range(n_slab):
        buf[s, 0:halo, :] = buf[s, tt:tt + halo, :]


def _sample_conv_kernel(full_ref, w_ref, b_ref, lg_ref, lb_ref, o_ref, *, n_new, n_seq, taps):
    for t in range(n_new):
        cv = jnp.broadcast_to(b_ref[...], (n_seq, full_ref.shape[2]))
        for j in range(taps):
            cv = cv + w_ref[j:j + 1, :] * full_ref[t + j]
        o_ref[t] = _silu(_ln(cv, lg_ref[...], lb_ref[...])).astype(o_ref.dtype)


def _sample_conv(full_tm, conv_w, conv_b, ln_g, ln_b, n_new):
    n_time, n_seq, width = full_tm.shape
    taps = conv_w.shape[0]
    return pl.pallas_call(
        functools.partial(_sample_conv_kernel, n_new=n_new, n_seq=n_seq, taps=taps),
        out_shape=jax.ShapeDtypeStruct((n_new, n_seq, width), BF16),
        compiler_params=pltpu.CompilerParams(vmem_limit_bytes=VMEM_LIMIT),
        name="sample_conv",
    )(full_tm, conv_w, conv_b.reshape(1, width), ln_g.reshape(1, width), ln_b.reshape(1, width))


def _ffn_kernel(*refs, has_state, tm, pad, tshift, fc, d_ff, alpha):
    (xp_ref, attn_ref, conv_ref, g1_ref, sh2_ref, sc2_ref, g2_ref,
     wout_ref, l1g_ref, l1b_ref, wup_ref, cw_ref, cb_ref, wdn_ref, l2g_ref, l2b_ref) = refs[:16]
    refs = refs[16:]
    if has_state:
        state_ref, refs = refs[0], refs[1:]
    y_ref, tail_ref, ubuf, acc_ref = refs
    half = attn_ref.shape[1]
    n_slab = fc // LANE

    if has_state:
        hist_ref = state_ref
    else:
        hist_ref = tail_ref

        @pl.when(pl.program_id(1) == 0)
        def _zero_history():
            tail_ref[...] = jnp.zeros(tail_ref.shape, F32)

    def mod_rows(ref):
        m = ref[...]
        reps = 1 if m.shape[0] == 1 else tm // m.shape[0]
        return m if reps == 1 else jnp.concatenate([m] * reps, axis=0)

    mix = _dot(attn_ref[...], wout_ref[0:half, :]) + _dot(conv_ref[...], wout_ref[half:, :])
    x1 = _ln(alpha * xp_ref[...] + mod_rows(g1_ref) * mix, l1g_ref[...], l1b_ref[...])
    h2 = (x1 * (1.0 + mod_rows(sc2_ref)) + mod_rows(sh2_ref)).astype(BF16)

    n_chunks = d_ff // fc

    def up_project(c):
        for part in range(2):
            col = part * d_ff + c * fc
            up = _dot(h2, wup_ref[:, col:col + fc])
            for s in range(n_slab):
                cols = slice(col + s * LANE, col + (s + 1) * LANE)
                slab = ((c % 2) * 2 + part) * n_slab + s
                ubuf[slab, 0:pad, :] = hist_ref[:, cols]
                ubuf[slab, pad:pad + tm, :] = up[:, s * LANE:(s + 1) * LANE]
            tail_ref[:, col:col + fc] = up[tm - pad:, :]

    def conv_half(c, part):
        parts = []
        for s in range(n_slab):
            col = part * d_ff + c * fc + s * LANE
            cols = slice(col, col + LANE)
            slab = ((c % 2) * 2 + part) * n_slab + s
            parts.append(cw_ref[0:1, cols] * ubuf[slab, pad - 2 * tshift:pad - 2 * tshift + tm, :]
                         + cw_ref[1:2, cols] * ubuf[slab, pad - tshift:pad - tshift + tm, :]
                         + cw_ref[2:3, cols] * ubuf[slab, pad:pad + tm, :] + cb_ref[:, cols])
        return jnp.concatenate(parts, axis=1)

    up_project(0)
    for c in range(n_chunks):
        if c + 1 < n_chunks:
            up_project(c + 1)
        g = (_silu(conv_half(c, 0)) * conv_half(c, 1)).astype(BF16)
        contrib = _dot(g, wdn_ref[c * fc:(c + 1) * fc, :])
        if c == 0:
            acc_ref[...] = contrib
        else:
            acc_ref[...] += contrib

    y_ref[...] = _ln(alpha * x1 + mod_rows(g2_ref) * acc_ref[...], l2g_ref[...], l2b_ref[...])


def _ffn(xp2d, attn, conv, mod, w_out_bf, ln1, w_up_bf, ffn_conv_w, ffn_conv_b,
         w_down_bf, ln2, *, groups, tiles, tm, pad, tshift, state, alpha, fc=MXU_TILE):
    n, d = xp2d.shape
    half = attn.shape[1]
    d_ff = w_down_bf.shape[0]
    r = mod.shape[1]
    has_state = state is not None
    assert pad == 2 * tshift or not has_state
    assert tm >= pad
    row = lambda w: pl.BlockSpec((tm, w), lambda b, t: (b * tiles + t, 0))
    mod_spec = lambda comp: pl.BlockSpec((None, r, d), lambda b, t: (b, 0, comp))
    vec = lambda a: a.reshape(1, -1)
    args = [xp2d, attn, conv, mod, mod, mod, mod,
            w_out_bf, vec(ln1[0]), vec(ln1[1]), w_up_bf, ffn_conv_w, vec(ffn_conv_b),
            w_down_bf, vec(ln2[0]), vec(ln2[1])]
    in_specs = [row(d), row(half), row(half), mod_spec(2), mod_spec(3), mod_spec(4), mod_spec(5),
                _const_spec(w_out_bf.shape), _const_spec((1, d)), _const_spec((1, d)),
                _const_spec(w_up_bf.shape), _const_spec(ffn_conv_w.shape),
                _const_spec((1, 2 * d_ff)), _const_spec(w_down_bf.shape),
                _const_spec((1, d)), _const_spec((1, d))]
    if has_state:
        args.append(state)
        in_specs.append(_const_spec(state.shape))
    y, tail = pl.pallas_call(
        functools.partial(_ffn_kernel, has_state=has_state, tm=tm, pad=pad,
                          tshift=tshift, fc=fc, d_ff=d_ff, alpha=alpha),
        grid=(groups, tiles),
        in_specs=in_specs,
        out_specs=[row(d), pl.BlockSpec((None, pad, 2 * d_ff), lambda b, t: (b, 0, 0))],
        out_shape=[jax.ShapeDtypeStruct((n, d), F32),
                   jax.ShapeDtypeStruct((groups, pad, 2 * d_ff), F32)],
        scratch_shapes=[pltpu.VMEM((4 * (fc // LANE), pad + tm, LANE), F32),
                        pltpu.VMEM((tm, d), F32)],
        compiler_params=pltpu.CompilerParams(
            dimension_semantics=("arbitrary", "arbitrary"), vmem_limit_bytes=VMEM_LIMIT),
        name="ffn",
    )(*args)
    return y, tail


def kernel(x_prompt, x_sample, c_prompt, c_sample, cache_k, cache_v, page_table, state_conv, state_ffn, ln_emb_g, ln_emb_b, w_ada, b_ada, w_in, lambda_q1, lambda_k1, lambda_q2, lambda_k2, subln_g, conv_w, conv_b, conv_ln_g, conv_ln_b, w_out, ln1_g, ln1_b, w_up, ffn_conv_w, ffn_conv_b, w_down, ln2_g, ln2_b):
    bsz, seq, d = x_prompt.shape
    n_seq, n_new, _ = x_sample.shape
    depth = w_ada.shape[0]
    assert depth == 1, "the prompt/sample activations are threaded for a single layer"
    n_phys, page = cache_k.shape[1], cache_k.shape[2]
    width = cache_k.shape[3] * cache_k.shape[4]
    d_ff = w_down.shape[1]
    conv_taps = conv_w.shape[1]
    ffn_taps = ffn_conv_w.shape[1]
    assert ffn_taps == 3
    alpha = (2 * depth) ** 0.25
    tm = ROW_TILE
    n_rows_s = n_seq * n_new
    n_kh = width // HEAD_DIM
    n_vh = width // HEAD_LANES

    l = 0
    lambda_init = 0.8 - 0.6 * math.exp(-0.3 * l)
    lam_params = (lambda_q1[l], lambda_k1[l], lambda_q2[l], lambda_k2[l])
    w_in_bf = w_in[l].astype(BF16)
    w_out_bf = w_out[l].astype(BF16)
    w_up_bf = w_up[l].astype(BF16)
    w_down_bf = w_down[l].astype(BF16)

    c_rows = bsz + n_seq
    c_all = jnp.concatenate([c_prompt, c_sample], axis=0)
    c_all = jnp.pad(c_all, ((0, -c_rows % (2 * SUBLANE)), (0, 0)))
    mod = _modulation(c_all, w_ada[l], b_ada[l])
    mod_p = mod[:bsz].reshape(bsz, 1, 6 * d)
    mod_s = mod[bsz:c_rows]

    pp = _in_projection(x_prompt.reshape(bsz * seq, d), mod_p, ln_emb_g, ln_emb_b, w_in_bf,
                        groups=bsz, tm=tm,
                        names=("xp", "qt_b", "k_b", "kt", "v_heads", "vt_b", "u"))
    u = pp["u"]

    xs_tm = x_sample.swapaxes(0, 1).reshape(n_rows_s, d)
    ps = _in_projection(xs_tm, mod_s[None], ln_emb_g, ln_emb_b, w_in_bf, groups=n_new, tm=n_seq,
                        names=("xp", "q", "k", "kt", "v", "u"))
    qs, ks, kts, vs, us = ps["q"], ps["k"], ps["kt"], ps["v"], ps["u"]
    to_sm = lambda a: a.reshape(n_new, n_seq, -1).swapaxes(0, 1)
    pad8 = lambda a: jnp.pad(to_sm(a), ((0, 0), (0, SUBLANE - n_new), (0, 0)))
    cache_kt = cache_k[l].transpose(0, 2, 3, 1).reshape(n_phys, width, page)
    cache_vr = cache_v[l].reshape(n_phys, page * n_vh, HEAD_LANES)
    attn_s8, conv = _sample_attention_and_prompt_conv(
        pad8(qs), pad8(ks), pad8(vs), cache_kt, cache_vr, page_table, lam_params, subln_g[l],
        n_new, lambda_init, u, conv_w[l], conv_b[l], conv_ln_g[l], conv_ln_b[l], seq)
    attn_s = attn_s8[:, :n_new, :].astype(BF16).swapaxes(0, 1).reshape(n_rows_s, width)

    attn = _prompt_attention(pp["qt_b"], pp["k_b"], pp["vt_b"], lam_params, subln_g[l],
                             lambda_init)
    y_p, tail_p = _ffn(pp["xp"], attn, conv, mod_p, w_out_bf,
                       (ln1_g[l], ln1_b[l]), w_up_bf, ffn_conv_w[l], ffn_conv_b[l], w_down_bf,
                       (ln2_g[l], ln2_b[l]), groups=bsz, tiles=seq // tm, tm=tm,
                       pad=SUBLANE, tshift=1, state=None, alpha=alpha)
    y_prompt = y_p.reshape(bsz, seq, d)
    k_prompt = pp["kt"].reshape(bsz, n_kh, HEAD_DIM, seq).transpose(0, 3, 1, 2)[None]
    v_prompt = pp["v_heads"].reshape(1, bsz, seq, n_vh, HEAD_LANES)
    conv_prompt = u.reshape(bsz, seq, -1)[None, :, seq - (conv_taps - 1):, :]
    ffn_prompt = tail_p[None, :, SUBLANE - (ffn_taps - 1):, :]

    full_tm = jnp.concatenate([state_conv[l].swapaxes(0, 1), us.reshape(n_new, n_seq, -1)], axis=0)
    conv_s = _sample_conv(full_tm, conv_w[l], conv_b[l], conv_ln_g[l], conv_ln_b[l],
                          n_new).reshape(n_rows_s, -1)
    state_tm = state_ffn[l].swapaxes(0, 1).reshape((ffn_taps - 1) * n_seq, 2 * d_ff)
    y_s, tail_s = _ffn(ps["xp"], attn_s, conv_s, mod_s[None],
                       w_out_bf, (ln1_g[l], ln1_b[l]), w_up_bf,
                       ffn_conv_w[l], ffn_conv_b[l], w_down_bf, (ln2_g[l], ln2_b[l]),
                       groups=1, tiles=1, tm=n_rows_s, pad=(ffn_taps - 1) * n_seq,
                       tshift=n_seq, state=state_tm, alpha=alpha)
    y_sample = y_s.reshape(n_new, n_seq, d).swapaxes(0, 1)
    k_sample = kts.reshape(n_new, n_kh, HEAD_DIM, n_seq).transpose(3, 0, 1, 2)[None]
    v_sample = to_sm(vs).reshape(1, n_seq, n_new, n_vh, HEAD_LANES)
    conv_sample = full_tm[n_new:].swapaxes(0, 1)[None]
    ffn_sample = tail_s.reshape(ffn_taps - 1, n_seq, 2 * d_ff).swapaxes(0, 1)[None]

    return (y_prompt, y_sample, k_prompt, v_prompt, conv_prompt, ffn_prompt,
            k_sample, v_sample, conv_sample, ffn_sample)
```
